```python
import jax, jax.numpy as jnp
from jax import lax
import numpy as np

D_MODEL = 1024
BATCH = 8
SEQ = 8192
DEPTH = 2

GRID_W = 64
CTX_LEN = 256
N_MIXERS = 2
N_A_LAYERS = (DEPTH + 1) // 2
N_B_LAYERS = DEPTH // 2
CHUNK = 128
GM_HALF = 3 * D_MODEL
GM_GROUPS = 8
GM_GROUP_DIM = GM_HALF // GM_GROUPS
HG_HEADS = 8
HG_KEY_DIM = D_MODEL // HG_HEADS
HG_VAL_DIM = D_MODEL // HG_HEADS
HG_KEY = HG_HEADS * HG_KEY_DIM
HG_VAL = HG_HEADS * HG_VAL_DIM
HG_PROJ = 3 * HG_KEY + 2 * HG_VAL
SCAN_CHUNK = 64
D_FF = ((8 * D_MODEL // 3 + 255) // 256) * 256
NORM_EPS = 1e-6
POS_BASE = 10000.0

kernel_name = 'hybrid_gmlp_hgrn2_prefix_dit'


def rms_norm(x, w):
    xf = x.astype(jnp.float32)
    xf = xf * lax.rsqrt(jnp.mean(xf * xf, axis=-1, keepdims=True) + NORM_EPS)
    return (xf * w.astype(jnp.float32)).astype(x.dtype)


def layer_norm(x, g, b):
    xf = x.astype(jnp.float32)
    mu = jnp.mean(xf, axis=-1, keepdims=True)
    xc = xf - mu
    var = jnp.mean(xc * xc, axis=-1, keepdims=True)
    return (xc * lax.rsqrt(var + NORM_EPS) * g.astype(jnp.float32) + b.astype(jnp.float32)).astype(x.dtype)


def ada_modulate(x, w, shift, scale):
    return rms_norm(x, w) * (1.0 + scale) + shift


def sincos(pos, dim):
    half = dim // 2
    omega = 1.0 / (POS_BASE ** (jnp.arange(half, dtype=jnp.float32) / half))
    ang = pos.astype(jnp.float32)[:, None] * omega[None, :]
    return jnp.concatenate([jnp.sin(ang), jnp.cos(ang)], axis=-1)


def grid_pos_code(n):
    rows = n // GRID_W
    half = D_MODEL // 2
    row_code = sincos(jnp.arange(rows), half)
    col_code = sincos(jnp.arange(GRID_W), half)
    code = jnp.concatenate([
        jnp.broadcast_to(row_code[:, None, :], (rows, GRID_W, half)),
        jnp.broadcast_to(col_code[None, :, :], (rows, GRID_W, half))], axis=-1)
    return code.reshape(rows * GRID_W, D_MODEL)


def chunk_gmlp(h, w_in, b_in, ln_g, ln_b, w_s, b_s, w_out):
    bsz, n, _ = h.shape
    z = jax.nn.gelu(h @ w_in + b_in, approximate=False)
    u, v = jnp.split(z, 2, axis=-1)
    v = layer_norm(v, ln_g, ln_b)
    v = v.reshape(bsz, n // CHUNK, CHUNK, GM_GROUPS, GM_GROUP_DIM)
    v = jnp.einsum('gpq,bcqgd->bcpgd', w_s, v) + b_s.T[:, :, None]
    v = v.reshape(bsz, n, GM_HALF)
    return (u * v) @ w_out


def gla_chunked(q, k, v, log_f, s0):
    bsz, n, h, _ = q.shape
    dv = v.shape[-1]
    nc = n // SCAN_CHUNK

    def blocks(t):
        return t.astype(jnp.float32).reshape(bsz, nc, SCAN_CHUNK, h, t.shape[-1])

    q, k, v, log_f = blocks(q), blocks(k), blocks(v), blocks(log_f)
    b = jnp.cumsum(log_f, axis=2)
    b_last = b[:, :, -1:]
    q_dec = q * jnp.exp(b)
    k_inv = k * jnp.exp(-b)
    k_dec = k * jnp.exp(b_last - b)
    lower = jnp.tril(jnp.ones((SCAN_CHUNK, SCAN_CHUNK), dtype=bool))
    att = jnp.einsum('bcthk,bcshk->bchts', q_dec, k_inv)
    att = jnp.where(lower, att, 0.0)
    o_intra = jnp.einsum('bchts,bcshv->bcthv', att, v)

    def step(state, xs):
        q_c, k_c, v_c, d_c = xs
        o_c = jnp.einsum('bthk,bhkv->bthv', q_c, state)
        state = d_c[..., None] * state + jnp.einsum('bshk,bshv->bhkv', k_c, v_c)
        return state, o_c

    xs = (jnp.moveaxis(q_dec, 1, 0), jnp.moveaxis(k_dec, 1, 0), jnp.moveaxis(v, 1, 0),
          jnp.moveaxis(jnp.exp(b_last[:, :, 0]), 1, 0))
    s_final, o_inter = lax.scan(step, s0, xs)
    o = o_intra + jnp.moveaxis(o_inter, 0, 1)
    return o.reshape(bsz, n, h, dv), s_final


def scan_final_state(k, v, log_f):
    b = jnp.cumsum(log_f.astype(jnp.float32), axis=1)
    k_dec = k.astype(jnp.float32) * jnp.exp(b[:, -1:] - b)
    return jnp.einsum('bshk,bshv->bhkv', k_dec, v.astype(jnp.float32))


def hg_forget(z, lb):
    zf = z.astype(jnp.float32)
    log_f = jnp.logaddexp(jnp.log(lb), jnp.log1p(-lb) + jax.nn.log_sigmoid(zf))
    key = (1.0 - lb) * jax.nn.sigmoid(-zf)
    return key, log_f


def hgrn2_mixer(h_lat, h_ctx, w_in, lower_bound, norm_w, w_out, ctx_out):
    bsz = h_lat.shape[0]
    lbs = lower_bound.astype(jnp.float32).reshape(2, HG_HEADS, HG_KEY_DIM)
    cuts = [HG_KEY, 2 * HG_KEY, 3 * HG_KEY, 3 * HG_KEY + HG_VAL]

    def heads(t, d):
        return t.reshape(t.shape[0], t.shape[1], HG_HEADS, d)

    def rev(t):
        return jnp.flip(t, axis=1)

    def readout(o, g):
        o = rms_norm(o, norm_w.reshape(HG_HEADS, HG_VAL_DIM))
        o = o.reshape(o.shape[0], o.shape[1], HG_VAL).astype(g.dtype) * jax.nn.silu(g)
        return o @ w_out

    q_x, ff_x, fb_x, i_x, g_x = jnp.split(h_lat @ w_in, cuts, axis=-1)
    q_x = heads(jax.nn.silu(q_x), HG_KEY_DIM)
    i_x = heads(i_x, HG_VAL_DIM)
    kf_x, lf_x = hg_forget(heads(ff_x, HG_KEY_DIM), lbs[0])
    kb_x, lbk_x = hg_forget(heads(fb_x, HG_KEY_DIM), lbs[1])

    if ctx_out:
        q_c, ff_c, fb_c, i_c, g_c = jnp.split(h_ctx @ w_in, cuts, axis=-1)
        q_c = heads(jax.nn.silu(q_c), HG_KEY_DIM)
        i_c = heads(i_c, HG_VAL_DIM)
        kf_c, lf_c = hg_forget(heads(ff_c, HG_KEY_DIM), lbs[0])
        kb_c, lbk_c = hg_forget(heads(fb_c, HG_KEY_DIM), lbs[1])
        zero = jnp.zeros((bsz, HG_HEADS, HG_KEY_DIM, HG_VAL_DIM), jnp.float32)
        o_cf, s_f = gla_chunked(q_c, kf_c, i_c, lf_c, zero)
        o_cb, s_b = gla_chunked(rev(q_c), rev(kb_c), rev(i_c), rev(lbk_c), zero)
        y_ctx = readout(o_cf + rev(o_cb), g_c)
    else:
        ff_c, fb_c, i_c = jnp.split(h_ctx @ w_in[:, HG_KEY:3 * HG_KEY + HG_VAL], [HG_KEY, 2 * HG_KEY], axis=-1)
        i_c = heads(i_c, HG_VAL_DIM)
        kf_c, lf_c = hg_forget(heads(ff_c, HG_KEY_DIM), lbs[0])
        kb_c, lbk_c = hg_forget(heads(fb_c, HG_KEY_DIM), lbs[1])
        s_f = scan_final_state(kf_c, i_c, lf_c)
        s_b = scan_final_state(rev(kb_c), rev(i_c), rev(lbk_c))
        y_ctx = None

    o_f, _ = gla_chunked(q_x, kf_x, i_x, lf_x, s_f)
    o_b, _ = gla_chunked(rev(q_x), rev(kb_x), rev(i_x), rev(lbk_x), s_b)
    y_lat = readout(o_f + rev(o_b), g_x)
    return y_lat, y_ctx


def swiglu(h, w_in, w_out):
    a, b = jnp.split(h @ w_in, 2, axis=-1)
    return (jax.nn.silu(a) * b) @ w_out


def setup_inputs(seed: int = 0) -> dict:
    key = jax.random.key(seed)
    ks = iter(jax.random.split(key, 22))
    D = D_MODEL

    def nrm(shape, s):
        return jax.random.normal(next(ks), shape, jnp.float32) * s

    return {
        'x': nrm((BATCH, SEQ, D), 1.0),
        'c': nrm((BATCH, D), 1.0),
        'ctx': nrm((BATCH, CTX_LEN, D), 1.0),
        'c_ctx': nrm((D,), 1.0),
        'ada_w': nrm((DEPTH, D, 6 * D), D ** -0.5),
        'ada_b': nrm((DEPTH, 6 * D), 0.01),
        'norm_mix_w': 1.0 + nrm((DEPTH, D), 0.02),
        'norm_ffn_w': 1.0 + nrm((DEPTH, D), 0.02),
        'gm_w_in': nrm((N_A_LAYERS, D, 2 * GM_HALF), D ** -0.5),
        'gm_b_in': nrm((N_A_LAYERS, 2 * GM_HALF), 0.01),
        'gm_ln_g': 1.0 + nrm((N_A_LAYERS, GM_HALF), 0.02),
        'gm_ln_b': nrm((N_A_LAYERS, GM_HALF), 0.01),
        'gm_w_s': nrm((N_A_LAYERS, GM_GROUPS, CHUNK, CHUNK), CHUNK ** -0.5),
        'gm_b_s': 1.0 + nrm((N_A_LAYERS, GM_GROUPS, CHUNK), 0.02),
        'gm_w_out': nrm((N_A_LAYERS, GM_HALF, D), GM_HALF ** -0.5),
        'hg_w_in': nrm((N_B_LAYERS, D, HG_PROJ), D ** -0.5),
        'hg_lb': nrm((DEPTH, 2, HG_KEY), 0.1),
        'hg_norm_w': 1.0 + nrm((N_B_LAYERS, HG_VAL), 0.02),
        'hg_w_out': nrm((N_B_LAYERS, HG_VAL, D), HG_VAL ** -0.5),
        'ffn_w_in': nrm((DEPTH, D, 2 * D_FF), D ** -0.5),
        'ffn_w_out': nrm((DEPTH, D_FF, D), D_FF ** -0.5),
        'final_norm_w': 1.0 + nrm((D,), 0.02),
    }


def reference(x, c, ctx, c_ctx, ada_w, ada_b, norm_mix_w, norm_ffn_w, gm_w_in, gm_b_in, gm_ln_g, gm_ln_b,
              gm_w_s, gm_b_s, gm_w_out, hg_w_in, hg_lb, hg_norm_w, hg_w_out, ffn_w_in, ffn_w_out, final_norm_w):
    n = x.shape[1]
    x = x + grid_pos_code(n).astype(x.dtype)
    p = jax.nn.softmax(hg_lb.astype(jnp.float32), axis=0)
    lower_bounds = jnp.cumsum(p, axis=0) - p[0]
    s_lat = jax.nn.silu(c)
    s_ctx = jax.nn.silu(c_ctx)

    for i in range(DEPTH):
        last = i == DEPTH - 1
        use_a = i % N_MIXERS == 0
        j = i // N_MIXERS
        ctx_needed = (not last) or (not use_a)

        mod_lat = (s_lat @ ada_w[i] + ada_b[i])[:, None, :]
        sh_m, sc_m, gt_m, sh_f, sc_f, gt_f = jnp.split(mod_lat, 6, axis=-1)
        h_lat = ada_modulate(x, norm_mix_w[i], sh_m, sc_m)
        if ctx_needed:
            mod_ctx = s_ctx @ ada_w[i] + ada_b[i]
            csh_m, csc_m, cgt_m, csh_f, csc_f, cgt_f = jnp.split(mod_ctx, 6, axis=-1)
            h_ctx = ada_modulate(ctx, norm_mix_w[i], csh_m, csc_m)

        if use_a:
            gm = (gm_w_in[j], gm_b_in[j], gm_ln_g[j], gm_ln_b[j], gm_w_s[j], gm_b_s[j], gm_w_out[j])
            y_lat = chunk_gmlp(h_lat, *gm)
            y_ctx = None if last else chunk_gmlp(h_ctx, *gm)
        else:
            y_lat, y_ctx = hgrn2_mixer(h_lat, h_ctx, hg_w_in[j], lower_bounds[i], hg_norm_w[j],
                                       hg_w_out[j], not last)

        x = x + gt_m * y_lat
        x = x + gt_f * swiglu(ada_modulate(x, norm_ffn_w[i], sh_f, sc_f), ffn_w_in[i], ffn_w_out[i])
        if not last:
            ctx = ctx + cgt_m * y_ctx
            ctx = ctx + cgt_f * swiglu(ada_modulate(ctx, norm_ffn_w[i], csh_f, csc_f), ffn_w_in[i], ffn_w_out[i])

    return rms_norm(x, final_norm_w)
```

```python
import functools

import jax
import jax.numpy as jnp
from jax import lax
from jax.experimental import pallas as pl
from jax.experimental.pallas import tpu as pltpu

NORM_EPS = 1e-6
POS_BASE = 10000.0
GRID_W = 64
N_MIXERS = 2
GM_CHUNK = 128
GM_GROUPS = 8
HG_HEADS = 8
SCAN_CHUNK = 64
MOD_ROWS = 16

V7X_VMEM_LIMIT_BYTES = 56 * 1024 * 1024
COL_BLOCK = 256

BF16 = jnp.bfloat16
F32 = jnp.float32


def _dot(a, b):
    return jnp.dot(a, b, preferred_element_type=F32)


def _dot_nt(a, b):
    return lax.dot_general(a, b, (((1,), (1,)), ((), ())), preferred_element_type=F32)


def _dot_tn(a, b):
    return lax.dot_general(a, b, (((0,), (0,)), ((), ())), preferred_element_type=F32)


def _ada_norm(x, w, shift, scale):
    ms = jnp.mean(x * x, axis=-1, keepdims=True)
    return (x * lax.rsqrt(ms + NORM_EPS) * w) * (1.0 + scale) + shift


def _gelu_exact(x):
    return 0.5 * x * (1.0 + lax.erf(x * (2.0 ** -0.5)))


def _const_spec(shape):
    return pl.BlockSpec(shape, lambda *_: (0,) * len(shape), pipeline_mode=pl.Buffered(1))


def _ada_mod_kernel(c_ref, w_ref, b_ref, o_ref):
    c = c_ref[...]
    s = c * jax.nn.sigmoid(c)
    o_ref[...] = jnp.dot(s, w_ref[...], preferred_element_type=F32,
                         precision=lax.Precision.HIGHEST) + b_ref[...]


def _ada_mod(cvec, ada_w, ada_b):
    depth, d, n6 = ada_w.shape
    tn = 1536
    return pl.pallas_call(
        _ada_mod_kernel,
        out_shape=jax.ShapeDtypeStruct((depth, MOD_ROWS, n6), F32),
        grid=(depth, n6 // tn),
        in_specs=[
            pl.BlockSpec((MOD_ROWS, d), lambda l, j: (0, 0)),
            pl.BlockSpec((None, d, tn), lambda l, j: (l, 0, j)),
            pl.BlockSpec((None, 1, tn), lambda l, j: (l, 0, j)),
        ],
        out_specs=pl.BlockSpec((None, MOD_ROWS, tn), lambda l, j: (l, 0, j)),
        compiler_params=pltpu.CompilerParams(
            dimension_semantics=("arbitrary", "arbitrary"),
            vmem_limit_bytes=V7X_VMEM_LIMIT_BYTES),
        name="ada_mod",
    )(cvec, ada_w, ada_b.reshape(depth, 1, n6))


def _gmlp_kernel(add_pos, *refs):
    if add_pos:
        x_ref, pos_ref = refs[0], refs[1]
        refs = refs[2:]
    else:
        x_ref, pos_ref = refs[0], None
        refs = refs[1:]
    (mod_ref, nw_ref, win_ref, bin_ref, lng_ref, lnb_ref, ws_ref, bs_ref, wout_ref,
     o_ref, z_scr, p_scr) = refs
    tm, d = x_ref.shape
    half = lng_ref.shape[-1]
    gd = half // GM_GROUPS
    n_chunks = tm // GM_CHUNK

    x = x_ref[...]
    if add_pos:
        x = x + pos_ref[...]
    h = _ada_norm(x, nw_ref[...], mod_ref[0:1, :], mod_ref[1:2, :]).astype(BF16)

    for j in range(2 * half // COL_BLOCK):
        cs = slice(j * COL_BLOCK, (j + 1) * COL_BLOCK)
        z = _dot(h, win_ref[:, cs]) + bin_ref[:, cs]
        z_scr[:, cs] = _gelu_exact(z)

    v = z_scr[:, half:]
    mu = jnp.mean(v, axis=-1, keepdims=True)
    vc = v - mu
    rstd = lax.rsqrt(jnp.mean(vc * vc, axis=-1, keepdims=True) + NORM_EPS)

    for g in range(GM_GROUPS):
        ucs = slice(g * gd, (g + 1) * gd)
        vcs = slice(half + g * gd, half + (g + 1) * gd)
        vn = ((z_scr[:, vcs] - mu) * rstd * lng_ref[:, ucs] + lnb_ref[:, ucs]).astype(BF16)
        for c in range(n_chunks):
            rs = slice(c * GM_CHUNK, (c + 1) * GM_CHUNK)
            vs = _dot(ws_ref[g], vn[rs, :]) + bs_ref[g]
            p_scr[rs, ucs] = (z_scr[rs, ucs] * vs).astype(BF16)

    y = _dot(p_scr[...], wout_ref[...])
    o_ref[...] = x + mod_ref[2:3, :] * y


def _gmlp_mixer(x, pos, mod, mod_row, norm_w, w_in, b_in, ln_g, ln_b, w_s, b_s, w_out, tm):
    nb, n, d = x.shape
    half = ln_g.shape[-1]
    gd = half // GM_GROUPS
    add_pos = pos is not None
    mod_map = (lambda b, i: (b, 0, 0)) if mod_row is None else (lambda b, i: (mod_row, 0, 0))
    in_specs = [pl.BlockSpec((None, tm, d), lambda b, i: (b, i, 0))]
    args = [x]
    if add_pos:
        in_specs.append(pl.BlockSpec((tm, d), lambda b, i: (i, 0)))
        args.append(pos)
    in_specs += [
        pl.BlockSpec((None, 6, d), mod_map),
        _const_spec((1, d)),
        _const_spec((d, 2 * half)),
        _const_spec((1, 2 * half)),
        _const_spec((1, half)),
        _const_spec((1, half)),
        _const_spec((GM_GROUPS, GM_CHUNK, GM_CHUNK)),
        _const_spec((GM_GROUPS, GM_CHUNK, gd)),
        _const_spec((half, d)),
    ]
    args += [mod, norm_w.reshape(1, d), w_in, b_in.reshape(1, -1), ln_g.reshape(1, -1), ln_b.reshape(1, -1),
             w_s, b_s, w_out]
    return pl.pallas_call(
        functools.partial(_gmlp_kernel, add_pos),
        out_shape=jax.ShapeDtypeStruct((nb, n, d), F32),
        grid=(nb, n // tm),
        in_specs=in_specs,
        out_specs=pl.BlockSpec((None, tm, d), lambda b, i: (b, i, 0)),
        scratch_shapes=[pltpu.VMEM((tm, 2 * half), F32), pltpu.VMEM((tm, half), BF16)],
        compiler_params=pltpu.CompilerParams(
            dimension_semantics=("arbitrary", "arbitrary"),
            vmem_limit_bytes=V7X_VMEM_LIMIT_BYTES),
        name="gmlp_mixer",
    )(*args)


def _swiglu_residual(x, mod_ref, nw_ref, win_ref, wout_ref, g_scr):
    d_ff = g_scr.shape[-1]
    h = _ada_norm(x, nw_ref[...], mod_ref[3:4, :], mod_ref[4:5, :]).astype(BF16)
    for j in range(d_ff // COL_BLOCK):
        ab = _dot(h, win_ref[:, 2 * j * COL_BLOCK:(2 * j + 2) * COL_BLOCK])
        a = ab[:, :COL_BLOCK]
        b = ab[:, COL_BLOCK:]
        g_scr[:, j * COL_BLOCK:(j + 1) * COL_BLOCK] = (a * jax.nn.sigmoid(a) * b).astype(BF16)
    y = _dot(g_scr[...], wout_ref[...])
    return x + mod_ref[5:6, :] * y


def _rms_norm(x, w):
    ms = jnp.mean(x * x, axis=-1, keepdims=True)
    return x * lax.rsqrt(ms + NORM_EPS) * w


def _ffn_kernel(final_norm, x_ref, mod_ref, nw_ref, win_ref, wout_ref, *rest):
    if final_norm:
        fnw_ref, o_ref, g_scr = rest
    else:
        o_ref, g_scr = rest
    x = _swiglu_residual(x_ref[...], mod_ref, nw_ref, win_ref, wout_ref, g_scr)
    if final_norm:
        x = _rms_norm(x, fnw_ref[...])
    o_ref[...] = x


def _interleave_ffn_w_in(w_in):
    d, two_ff = w_in.shape
    d_ff = two_ff // 2
    w = w_in.reshape(d, 2, d_ff // COL_BLOCK, COL_BLOCK)
    return jnp.swapaxes(w, 1, 2).reshape(d, two_ff)


def _swiglu_ffn(x, mod, mod_row, norm_w, w_in_il, w_out, final_norm_w, tm):
    nb, n, d = x.shape
    d_ff = w_out.shape[0]
    final_norm = final_norm_w is not None
    mod_map = (lambda b, i: (b, 0, 0)) if mod_row is None else (lambda b, i: (mod_row, 0, 0))
    in_specs = [
        pl.BlockSpec((None, tm, d), lambda b, i: (b, i, 0)),
        pl.BlockSpec((None, 6, d), mod_map),
        _const_spec((1, d)),
        _const_spec((d, 2 * d_ff)),
        _const_spec((d_ff, d)),
    ]
    args = [x, mod, norm_w.reshape(1, d), w_in_il, w_out]
    if final_norm:
        in_specs.append(_const_spec((1, d)))
        args.append(final_norm_w.reshape(1, d))
    return pl.pallas_call(
        functools.partial(_ffn_kernel, final_norm),
        out_shape=jax.ShapeDtypeStruct((nb, n, d), F32),
        grid=(nb, n // tm),
        in_specs=in_specs,
        out_specs=pl.BlockSpec((None, tm, d), lambda b, i: (b, i, 0)),
        scratch_shapes=[pltpu.VMEM((tm, d_ff), BF16)],
        compiler_params=pltpu.CompilerParams(
            dimension_semantics=("arbitrary", "arbitrary"),
            vmem_limit_bytes=V7X_VMEM_LIMIT_BYTES),
        name="swiglu_ffn",
    )(*args)


def _lower_bounds(lb_ref, layer, depth):
    out = []
    for dirn in range(2):
        logits = [lb_ref[2 * l + dirn:2 * l + dirn + 1, :] for l in range(depth)]
        m = functools.reduce(jnp.maximum, logits)
        e = [jnp.exp(t - m) for t in logits]
        denom = functools.reduce(lambda a, b: a + b, e)
        num = jnp.zeros_like(m)
        for l in range(1, layer + 1):
            num = num + e[l]
        out.append(num / denom)
    return out


def _forget(z, lb):
    log_f = jnp.logaddexp(jnp.log(lb), jnp.log1p(-lb) + jax.nn.log_sigmoid(z))
    key = (1.0 - lb) * jax.nn.sigmoid(-z)
    return key, log_f


def _split3(x):
    hi = x.astype(BF16)
    r = x - hi.astype(F32)
    mid = r.astype(BF16)
    lo = (r - mid.astype(F32)).astype(BF16)
    return hi, mid, lo


def _scan_tile(q_scr, k_scr, lf_scr, v_scr, st_ref, o_ref, rev):
    t, hk = q_scr.shape
    dk = hk // HG_HEADS
    n_chunks = t // SCAN_CHUNK
    row = lax.broadcasted_iota(jnp.int32, (SCAN_CHUNK, SCAN_CHUNK), 0)
    col = lax.broadcasted_iota(jnp.int32, (SCAN_CHUNK, SCAN_CHUNK), 1)
    tri = (row <= col) if rev else (row >= col)
    tri_bf = tri.astype(BF16)
    last = 0 if rev else SCAN_CHUNK - 1
    order = range(n_chunks - 1, -1, -1) if rev else range(n_chunks)
    for c in order:
        rs = slice(c * SCAN_CHUNK, (c + 1) * SCAN_CHUNK)
        hi, mid, lo = _split3(lf_scr[rs, :])
        b = _dot(tri_bf, hi) + _dot(tri_bf, mid) + _dot(tri_bf, lo)
        b_last = b[last:last + 1, :]
        kf = k_scr[rs, :]
        q_dec = (q_scr[rs, :] * jnp.exp(b)).astype(BF16)
        k_inv = (kf * jnp.exp(-b)).astype(BF16)
        k_dec = (kf * jnp.exp(b_last - b)).astype(BF16)
        decay = jnp.exp(b_last)
        vv = v_scr[rs, :].astype(BF16)
        for hh in range(HG_HEADS):
            cs = slice(hh * dk, (hh + 1) * dk)
            att = _dot_nt(q_dec[:, cs], k_inv[:, cs])
            att = jnp.where(tri, att, 0.0).astype(BF16)
            st = st_ref[cs, :]
            o = _dot(att, vv[:, cs]) + _dot_nt(q_dec[:, cs], st.astype(BF16))
            st_ref[cs, :] = st * decay[:, cs] + _dot_tn(vv[:, cs], k_dec[:, cs])
            o_ref[rs, cs] = o.astype(o_ref.dtype)


def _hgrn2_scan_kernel(layer, depth, xf_ref, xb_ref, mod_ref, nw_ref, wq_ref, wff_ref, wfb_ref, wi_ref,
                       lb_ref, s0_ref, of_ref, ob_ref, sfin_ref, st_scr, q_scr, k_scr, lf_scr, v_scr):
    i = pl.program_id(1)

    @pl.when(i == 0)
    def _():
        st_scr[...] = s0_ref[...]

    lbs = _lower_bounds(lb_ref, layer, depth)
    for dirn, (x_ref, wf_ref, o_ref) in enumerate(((xf_ref, wff_ref, of_ref), (xb_ref, wfb_ref, ob_ref))):
        h = _ada_norm(x_ref[...], nw_ref[...], mod_ref[0:1, :], mod_ref[1:2, :]).astype(BF16)
        q = _dot(h, wq_ref[...])
        q_scr[...] = q * jax.nn.sigmoid(q)
        key, log_f = _forget(_dot(h, wf_ref[...]), lbs[dirn])
        k_scr[...] = key
        lf_scr[...] = log_f
        v_scr[...] = _dot(h, wi_ref[...])
        _scan_tile(q_scr, k_scr, lf_scr, v_scr, st_scr.at[dirn], o_ref, rev=(dirn == 1))

    @pl.when(i == pl.num_programs(1) - 1)
    def _():
        sfin_ref[...] = st_scr[...]


def _hgrn2_scan(x, mod, mod_row, norm_w, wq, wff, wfb, wi, hg_lb, layer, s0, t):
    nb, n, d = x.shape
    hk = wq.shape[1]
    dk = hk // HG_HEADS
    nt = n // t
    depth = hg_lb.shape[0]
    mod_map = (lambda b, i: (b, 0, 0)) if mod_row is None else (lambda b, i: (mod_row, 0, 0))
    lb2 = hg_lb.reshape(depth * 2, hk)
    return pl.pallas_call(
        functools.partial(_hgrn2_scan_kernel, layer, depth),
        out_shape=(jax.ShapeDtypeStruct((nb, n, hk), BF16),
                   jax.ShapeDtypeStruct((nb, n, hk), BF16),
                   jax.ShapeDtypeStruct((nb, 2, hk, dk), F32)),
        grid=(nb, nt),
        in_specs=[
            pl.BlockSpec((None, t, d), lambda b, i: (b, i, 0)),
            pl.BlockSpec((None, t, d), lambda b, i: (b, nt - 1 - i, 0)),
            pl.BlockSpec((None, 6, d), mod_map),
            _const_spec((1, d)),
            _const_spec((d, hk)),
            _const_spec((d, hk)),
            _const_spec((d, hk)),
            _const_spec((d, hk)),
            _const_spec((depth * 2, hk)),
            pl.BlockSpec((None, 2, hk, dk), lambda b, i: (b, 0, 0, 0)),
        ],
        out_specs=(
            pl.BlockSpec((None, t, hk), lambda b, i: (b, i, 0)),
            pl.BlockSpec((None, t, hk), lambda b, i: (b, nt - 1 - i, 0)),
            pl.BlockSpec((None, 2, hk, dk), lambda b, i: (b, 0, 0, 0)),
        ),
        scratch_shapes=[pltpu.VMEM((2, hk, dk), F32)] + [pltpu.VMEM((t, hk), F32)] * 4,
        compiler_params=pltpu.CompilerParams(
            dimension_semantics=("arbitrary", "arbitrary"),
            vmem_limit_bytes=V7X_VMEM_LIMIT_BYTES),
        name="hgrn2_scan",
    )(x, x, mod, norm_w.reshape(1, d), wq, wff, wfb, wi, lb2, s0)


def _hgrn2_ffn_kernel(final_norm, x_ref, of_ref, ob_ref, mod_ref, nwm_ref, wg_ref, hnw_ref, wo_ref,
                      nwf_ref, win_ref, wout_ref, *rest):
    if final_norm:
        fnw_ref, o_ref, g_scr = rest
    else:
        o_ref, g_scr = rest
    x = x_ref[...]
    hv = of_ref.shape[-1]
    dv = hv // HG_HEADS
    h = _ada_norm(x, nwm_ref[...], mod_ref[0:1, :], mod_ref[1:2, :]).astype(BF16)
    gate = _dot(h, wg_ref[...])
    gate = gate * jax.nn.sigmoid(gate)
    o = of_ref[...].astype(F32) + ob_ref[...].astype(F32)
    parts = []
    for hh in range(HG_HEADS):
        cs = slice(hh * dv, (hh + 1) * dv)
        oh = o[:, cs]
        ms = jnp.mean(oh * oh, axis=-1, keepdims=True)
        parts.append(oh * lax.rsqrt(ms + NORM_EPS) * hnw_ref[:, cs])
    on = jnp.concatenate(parts, axis=-1)
    y = _dot((on * gate).astype(BF16), wo_ref[...])
    x = x + mod_ref[2:3, :] * y
    x = _swiglu_residual(x, mod_ref, nwf_ref, win_ref, wout_ref, g_scr)
    if final_norm:
        ms = jnp.mean(x * x, axis=-1, keepdims=True)
        x = x * lax.rsqrt(ms + NORM_EPS) * fnw_ref[...]
    o_ref[...] = x


def _hgrn2_ffn(x, o_f, o_b, mod, mod_row, norm_mix_w, wg, hg_norm_w, hg_w_out, norm_ffn_w, w_in_il, w_out,
               final_norm_w, tm):
    nb, n, d = x.shape
    hv = o_f.shape[-1]
    d_ff = w_out.shape[0]
    final_norm = final_norm_w is not None
    mod_map = (lambda b, i: (b, 0, 0)) if mod_row is None else (lambda b, i: (mod_row, 0, 0))
    in_specs = [
        pl.BlockSpec((None, tm, d), lambda b, i: (b, i, 0)),
        pl.BlockSpec((None, tm, hv), lambda b, i: (b, i, 0)),
        pl.BlockSpec((None, tm, hv), lambda b, i: (b, i, 0)),
        pl.BlockSpec((None, 6, d), mod_map),
        _const_spec((1, d)),
        _const_spec((d, hv)),
        _const_spec((1, hv)),
        _const_spec((hv, d)),
        _const_spec((1, d)),
        _const_spec((d, 2 * d_ff)),
        _const_spec((d_ff, d)),
    ]
    args = [x, o_f, o_b, mod, norm_mix_w.reshape(1, d), wg, hg_norm_w.reshape(1, hv), hg_w_out,
            norm_ffn_w.reshape(1, d), w_in_il, w_out]
    if final_norm:
        in_specs.append(_const_spec((1, d)))
        args.append(final_norm_w.reshape(1, d))
    return pl.pallas_call(
        functools.partial(_hgrn2_ffn_kernel, final_norm),
        out_shape=jax.ShapeDtypeStruct((nb, n, d), F32),
        grid=(nb, n // tm),
        in_specs=in_specs,
        out_specs=pl.BlockSpec((None, tm, d), lambda b, i: (b, i, 0)),
        scratch_shapes=[pltpu.VMEM((tm, d_ff), BF16)],
        compiler_params=pltpu.CompilerParams(
            dimension_semantics=("arbitrary", "arbitrary"),
            vmem_limit_bytes=V7X_VMEM_LIMIT_BYTES),
        name="hgrn2_ffn",
    )(*args)


def _sincos(pos, dim):
    half = dim // 2
    omega = 1.0 / (POS_BASE ** (jnp.arange(half, dtype=F32) / half))
    ang = pos.astype(F32)[:, None] * omega[None, :]
    return jnp.concatenate([jnp.sin(ang), jnp.cos(ang)], axis=-1)


def _grid_pos_code(n, d):
    rows = n // GRID_W
    half = d // 2
    row_code = _sincos(jnp.arange(rows), half)
    col_code = _sincos(jnp.arange(GRID_W), half)
    code = jnp.concatenate([
        jnp.broadcast_to(row_code[:, None, :], (rows, GRID_W, half)),
        jnp.broadcast_to(col_code[None, :, :], (rows, GRID_W, half))], axis=-1)
    return code.reshape(rows * GRID_W, d)


def _pick_tile(n, target):
    t = min(n, target)
    while n % t:
        t //= 2
    return t


def kernel(x, c, ctx, c_ctx, ada_w, ada_b, norm_mix_w, norm_ffn_w, gm_w_in, gm_b_in, gm_ln_g, gm_ln_b,
           gm_w_s, gm_b_s, gm_w_out, hg_w_in, hg_lb, hg_norm_w, hg_w_out, ffn_w_in, ffn_w_out, final_norm_w):
    bsz, n, d = x.shape
    n_ctx = ctx.shape[1]
    depth = ada_w.shape[0]
    assert bsz + 1 <= MOD_ROWS
    hk = hg_lb.shape[-1]
    dk = hk // HG_HEADS

    cvec = jnp.zeros((MOD_ROWS, d), F32).at[:bsz].set(c).at[bsz].set(c_ctx)
    mod_all = _ada_mod(cvec, ada_w, ada_b).reshape(depth, MOD_ROWS, 6, d)
    ctx_row = bsz

    pos = _grid_pos_code(n, d)
    ctx_flat = ctx.reshape(1, bsz * n_ctx, d)

    tm_lat = _pick_tile(n, 256)
    tm_ctx = _pick_tile(n_ctx, 256)
    t_scan = _pick_tile(n, 256)

    for i in range(depth):
        last = i == depth - 1
        use_a = i % N_MIXERS == 0
        j = i // N_MIXERS
        mod = mod_all[i]
        ffn_in = _interleave_ffn_w_in(ffn_w_in[i]).astype(BF16)
        ffn_out = ffn_w_out[i].astype(BF16)

        if use_a:
            half = gm_ln_g.shape[-1]
            gd = half // GM_GROUPS
            gm = (norm_mix_w[i], gm_w_in[j].astype(BF16), gm_b_in[j], gm_ln_g[j], gm_ln_b[j],
                  gm_w_s[j].astype(BF16),
                  jnp.broadcast_to(gm_b_s[j][:, :, None], (GM_GROUPS, GM_CHUNK, gd)),
                  gm_w_out[j].astype(BF16))
            x = _gmlp_mixer(x, pos if i == 0 else None, mod, None, *gm, tm=tm_lat)
            x = _swiglu_ffn(x, mod, None, norm_ffn_w[i], ffn_in, ffn_out,
                            final_norm_w if last else None, tm=tm_lat)
            if not last:
                ctx_flat = _gmlp_mixer(ctx_flat, None, mod, ctx_row, *gm, tm=tm_ctx)
                ctx_flat = _swiglu_ffn(ctx_flat, mod, ctx_row, norm_ffn_w[i], ffn_in, ffn_out, None, tm=tm_ctx)
        else:
            if i == 0:
                x = x + pos
            w = hg_w_in[j].astype(BF16)
            wq, wff, wfb, wi, wg = (w[:, 0:hk], w[:, hk:2 * hk], w[:, 2 * hk:3 * hk], w[:, 3 * hk:4 * hk],
                                    w[:, 4 * hk:])
            scan = functools.partial(_hgrn2_scan, norm_w=norm_mix_w[i], wq=wq, wff=wff, wfb=wfb, wi=wi,
                                     hg_lb=hg_lb, layer=i)
            zero = jnp.zeros((bsz, 2, hk, dk), F32)
            ctx3 = ctx_flat.reshape(bsz, n_ctx, d)
            oc_f, oc_b, s_ctx = scan(ctx3, mod, ctx_row, s0=zero, t=n_ctx)
            o_f, o_b, _ = scan(x, mod, None, s0=s_ctx, t=t_scan)
            readout = functools.partial(_hgrn2_ffn, norm_mix_w=norm_mix_w[i], wg=wg, hg_norm_w=hg_norm_w[j],
                                        hg_w_out=hg_w_out[j].astype(BF16), norm_ffn_w=norm_ffn_w[i],
                                        w_in_il=ffn_in, w_out=ffn_out)
            x = readout(x, o_f, o_b, mod, None, final_norm_w=final_norm_w if last else None, tm=tm_lat)
            if not last:
                ctx3 = readout(ctx3, oc_f, oc_b, mod, ctx_row, final_norm_w=None, tm=tm_ctx)
                ctx_flat = ctx3.reshape(1, bsz * n_ctx, d)
    return x
```

```python
import functools

import jax
import jax.numpy as jnp
from jax import lax
from jax.experimental import pallas as pl
from jax.experimental.pallas import tpu as pltpu

NORM_EPS = 1e-6
POS_BASE = 10000.0
GRID_W = 64
N_MIXERS = 2
GM_CHUNK = 128
GM_GROUPS = 8
HG_HEADS = 8
SCAN_CHUNK = 64
MOD_ROWS = 16

V7X_VMEM_LIMIT_BYTES = 56 * 1024 * 1024
COL_BLOCK = 256

BF16 = jnp.bfloat16
F32 = jnp.float32


def _dot(a, b):
    return jnp.dot(a, b, preferred_element_type=F32)


def _dot_nt(a, b):
    return lax.dot_general(a, b, (((1,), (1,)), ((), ())), preferred_element_type=F32)


def _dot_tn(a, b):
    return lax.dot_general(a, b, (((0,), (0,)), ((), ())), preferred_element_type=F32)


def _ada_norm(x, w, shift, scale):
    ms = jnp.mean(x * x, axis=-1, keepdims=True)
    return (x * lax.rsqrt(ms + NORM_EPS) * w) * (1.0 + scale) + shift


def _gelu_exact(x):
    return 0.5 * x * (1.0 + lax.erf(x * (2.0 ** -0.5)))


def _const_spec(shape):
    return pl.BlockSpec(shape, lambda *_: (0,) * len(shape), pipeline_mode=pl.Buffered(1))


def _ada_mod_kernel(c_ref, w_ref, b_ref, o_ref):
    c = c_ref[...]
    s = c * jax.nn.sigmoid(c)
    o_ref[...] = jnp.dot(s, w_ref[...], preferred_element_type=F32,
                         precision=lax.Precision.HIGHEST) + b_ref[...]


def _ada_mod(cvec, ada_w, ada_b):
    depth, d, n6 = ada_w.shape
    tn = 1536
    return pl.pallas_call(
        _ada_mod_kernel,
        out_shape=jax.ShapeDtypeStruct((depth, MOD_ROWS, n6), F32),
        grid=(depth, n6 // tn),
        in_specs=[
            pl.BlockSpec((MOD_ROWS, d), lambda l, j: (0, 0)),
            pl.BlockSpec((None, d, tn), lambda l, j: (l, 0, j)),
            pl.BlockSpec((None, 1, tn), lambda l, j: (l, 0, j)),
        ],
        out_specs=pl.BlockSpec((None, MOD_ROWS, tn), lambda l, j: (l, 0, j)),
        compiler_params=pltpu.CompilerParams(
            dimension_semantics=("arbitrary", "arbitrary"),
            vmem_limit_bytes=V7X_VMEM_LIMIT_BYTES),
        name="ada_mod",
    )(cvec, ada_w, ada_b.reshape(depth, 1, n6))


def _gmlp_kernel(add_pos, *refs):
    if add_pos:
        x_ref, pos_ref = refs[0], refs[1]
        refs = refs[2:]
    else:
        x_ref, pos_ref = refs[0], None
        refs = refs[1:]
    (mod_ref, nw_ref, win_ref, bin_ref, lng_ref, lnb_ref, ws_ref, bs_ref, wout_ref,
     o_ref, h_scr, z_scr, vn_scr, p_scr) = refs
    tm, d = x_ref.shape
    half = lng_ref.shape[-1]
    gd = half // GM_GROUPS
    n_chunks = tm // GM_CHUNK

    x = x_ref[...]
    if add_pos:
        x = x + pos_ref[...]
    h_scr[...] = _ada_norm(x, nw_ref[...], mod_ref[0:1, :], mod_ref[1:2, :]).astype(BF16)

    for j in range(2 * half // COL_BLOCK):
        cs = slice(j * COL_BLOCK, (j + 1) * COL_BLOCK)
        z = _dot(h_scr[...], win_ref[:, cs]) + bin_ref[:, cs]
        z_scr[:, cs] = _gelu_exact(z)

    v = z_scr[:, half:]
    mu = jnp.mean(v, axis=-1, keepdims=True)
    vc = v - mu
    rstd = lax.rsqrt(jnp.mean(vc * vc, axis=-1, keepdims=True) + NORM_EPS)

    for g in range(GM_GROUPS):
        ucs = slice(g * gd, (g + 1) * gd)
        vcs = slice(half + g * gd, half + (g + 1) * gd)
        vn_scr[:, ucs] = ((z_scr[:, vcs] - mu) * rstd * lng_ref[:, ucs] + lnb_ref[:, ucs]).astype(BF16)
        for c in range(n_chunks):
            rs = slice(c * GM_CHUNK, (c + 1) * GM_CHUNK)
            vs = _dot(ws_ref[g], vn_scr[rs, ucs]) + bs_ref[g]
            p_scr[rs, ucs] = (z_scr[rs, ucs] * vs).astype(BF16)

    y = _dot(p_scr[...], wout_ref[...])
    o_ref[...] = x + mod_ref[2:3, :] * y


def _gmlp_mixer(x, pos, mod, mod_row, norm_w, w_in, b_in, ln_g, ln_b, w_s, b_s, w_out, tm):
    nb, n, d = x.shape
    half = ln_g.shape[-1]
    gd = half // GM_GROUPS
    add_pos = pos is not None
    mod_map = (lambda b, i: (b, 0, 0)) if mod_row is None else (lambda b, i: (mod_row, 0, 0))
    in_specs = [pl.BlockSpec((None, tm, d), lambda b, i: (b, i, 0))]
    args = [x]
    if add_pos:
        in_specs.append(pl.BlockSpec((tm, d), lambda b, i: (i, 0)))
        args.append(pos)
    in_specs += [
        pl.BlockSpec((None, 6, d), mod_map),
        _const_spec((1, d)),
        _const_spec((d, 2 * half)),
        _const_spec((1, 2 * half)),
        _const_spec((1, half)),
        _const_spec((1, half)),
        _const_spec((GM_GROUPS, GM_CHUNK, GM_CHUNK)),
        _const_spec((GM_GROUPS, GM_CHUNK, gd)),
        _const_spec((half, d)),
    ]
    args += [mod, norm_w.reshape(1, d), w_in, b_in.reshape(1, -1), ln_g.reshape(1, -1), ln_b.reshape(1, -1),
             w_s, b_s, w_out]
    return pl.pallas_call(
        functools.partial(_gmlp_kernel, add_pos),
        out_shape=jax.ShapeDtypeStruct((nb, n, d), F32),
        grid=(nb, n // tm),
        in_specs=in_specs,
        out_specs=pl.BlockSpec((None, tm, d), lambda b, i: (b, i, 0)),
        scratch_shapes=[pltpu.VMEM((tm, d), BF16), pltpu.VMEM((tm, 2 * half), F32),
                        pltpu.VMEM((tm, half), BF16), pltpu.VMEM((tm, half), BF16)],
        compiler_params=pltpu.CompilerParams(
            dimension_semantics=("arbitrary", "arbitrary"),
            vmem_limit_bytes=V7X_VMEM_LIMIT_BYTES),
        name="gmlp_mixer",
    )(*args)


def _swiglu_residual(x, mod_ref, nw_ref, win_ref, wout_ref, h_scr, g_scr):
    d_ff = g_scr.shape[-1]
    h_scr[...] = _ada_norm(x, nw_ref[...], mod_ref[3:4, :], mod_ref[4:5, :]).astype(BF16)
    for j in range(d_ff // COL_BLOCK):
        ab = _dot(h_scr[...], win_ref[:, 2 * j * COL_BLOCK:(2 * j + 2) * COL_BLOCK])
        a = ab[:, :COL_BLOCK]
        b = ab[:, COL_BLOCK:]
        g_scr[:, j * COL_BLOCK:(j + 1) * COL_BLOCK] = (a * jax.nn.sigmoid(a) * b).astype(BF16)
    y = _dot(g_scr[...], wout_ref[...])
    return x + mod_ref[5:6, :] * y


def _rms_norm(x, w):
    ms = jnp.mean(x * x, axis=-1, keepdims=True)
    return x * lax.rsqrt(ms + NORM_EPS) * w


def _ffn_kernel(final_norm, x_ref, mod_ref, nw_ref, win_ref, wout_ref, *rest):
    if final_norm:
        fnw_ref, o_ref, h_scr, g_scr = rest
    else:
        o_ref, h_scr, g_scr = rest
    x = _swiglu_residual(x_ref[...], mod_ref, nw_ref, win_ref, wout_ref, h_scr, g_scr)
    if final_norm:
        x = _rms_norm(x, fnw_ref[...])
    o_ref[...] = x


def _interleave_ffn_w_in(w_in):
    d, two_ff = w_in.shape
    d_ff = two_ff // 2
    w = w_in.reshape(d, 2, d_ff // COL_BLOCK, COL_BLOCK)
    return jnp.swapaxes(w, 1, 2).reshape(d, two_ff)


def _swiglu_ffn(x, mod, mod_row, norm_w, w_in_il, w_out, final_norm_w, tm):
    nb, n, d = x.shape
    d_ff = w_out.shape[0]
    final_norm = final_norm_w is not None
    mod_map = (lambda b, i: (b, 0, 0)) if mod_row is None else (lambda b, i: (mod_row, 0, 0))
    in_specs = [
        pl.BlockSpec((None, tm, d), lambda b, i: (b, i, 0)),
        pl.BlockSpec((None, 6, d), mod_map),
        _const_spec((1, d)),
        _const_spec((d, 2 * d_ff)),
        _const_spec((d_ff, d)),
    ]
    args = [x, mod, norm_w.reshape(1, d), w_in_il, w_out]
    if final_norm:
        in_specs.append(_const_spec((1, d)))
        args.append(final_norm_w.reshape(1, d))
    return pl.pallas_call(
        functools.partial(_ffn_kernel, final_norm),
        out_shape=jax.ShapeDtypeStruct((nb, n, d), F32),
        grid=(nb, n // tm),
        in_specs=in_specs,
        out_specs=pl.BlockSpec((None, tm, d), lambda b, i: (b, i, 0)),
        scratch_shapes=[pltpu.VMEM((tm, d), BF16), pltpu.VMEM((tm, d_ff), BF16)],
        compiler_params=pltpu.CompilerParams(
            dimension_semantics=("arbitrary", "arbitrary"),
            vmem_limit_bytes=V7X_VMEM_LIMIT_BYTES),
        name="swiglu_ffn",
    )(*args)


def _lower_bounds(lb_ref, layer, depth):
    out = []
    for dirn in range(2):
        logits = [lb_ref[2 * l + dirn:2 * l + dirn + 1, :] for l in range(depth)]
        m = functools.reduce(jnp.maximum, logits)
        e = [jnp.exp(t - m) for t in logits]
        denom = functools.reduce(lambda a, b: a + b, e)
        num = jnp.zeros_like(m)
        for l in range(1, layer + 1):
            num = num + e[l]
        out.append(num / denom)
    return out


LOG2_E = 1.4426950408889634


def _forget(z, lb):
    t = jnp.exp(-jnp.abs(z))
    r = 1.0 / (1.0 + t)
    tr = t * r
    pos = z >= 0.0
    one_m = 1.0 - lb
    f = lb + one_m * jnp.where(pos, r, tr)
    key = one_m * jnp.where(pos, tr, r)
    return key, jnp.log(f) * LOG2_E


def _scan_tile(q_scr, k_scr, lf_scr, v_scr, hm_scr, qd_scr, ki_scr, kd_scr, dec_scr, att_scr, u_scr, sb_scr,
               st_ref, o_ref, rev):
    t, hk = q_scr.shape
    dk = hk // HG_HEADS
    n_chunks = t // SCAN_CHUNK
    row = lax.broadcasted_iota(jnp.int32, (SCAN_CHUNK, SCAN_CHUNK), 0)
    col = lax.broadcasted_iota(jnp.int32, (SCAN_CHUNK, SCAN_CHUNK), 1)
    tri = (row <= col) if rev else (row >= col)
    tri_bf = tri.astype(BF16)
    last = 0 if rev else SCAN_CHUNK - 1
    order = range(n_chunks - 1, -1, -1) if rev else range(n_chunks)
    rows = [slice(c * SCAN_CHUNK, (c + 1) * SCAN_CHUNK) for c in range(n_chunks)]
    cols = [slice(hh * dk, (hh + 1) * dk) for hh in range(HG_HEADS)]

    def decays():
        for c in order:
            rs = rows[c]
            lf = lf_scr[rs, :]
            hi = lf.astype(BF16)
            hm_scr[rs, :hk] = hi
            hm_scr[rs, hk:] = (lf - hi.astype(F32)).astype(BF16)
            b2 = _dot(tri_bf, hm_scr[rs, :])
            b = b2[:, :hk] + b2[:, hk:]
            b_last = b[last:last + 1, :]
            kf = k_scr[rs, :]
            qd_scr[rs, :] = (q_scr[rs, :] * jnp.exp2(b)).astype(BF16)
            ki_scr[rs, :] = (kf * jnp.exp2(-b)).astype(BF16)
            kd_scr[rs, :] = (kf * jnp.exp2(b_last - b)).astype(BF16)
            dec_scr[c:c + 1, :] = jnp.exp2(b_last)

    def scores():
        for c in order:
            rs = rows[c]
            for hh, cs in enumerate(cols):
                att = _dot_nt(qd_scr[rs, cs], ki_scr[rs, cs])
                att_scr[rs, hh * dk:hh * dk + SCAN_CHUNK] = jnp.where(tri, att, 0.0).astype(BF16)
                u_scr[c, cs, :] = _dot_tn(v_scr[rs, cs], kd_scr[rs, cs])

    def recurrence():
        for c in order:
            for cs in cols:
                st = st_ref[cs, :]
                sb_scr[c, cs, :] = st.astype(BF16)
                st_ref[cs, :] = st * dec_scr[c:c + 1, cs] + u_scr[c, cs, :]

    def outputs():
        for c in order:
            rs = rows[c]
            for hh, cs in enumerate(cols):
                o = (_dot(att_scr[rs, hh * dk:hh * dk + SCAN_CHUNK], v_scr[rs, cs])
                     + _dot_nt(qd_scr[rs, cs], sb_scr[c, cs, :]))
                o_ref[rs, cs] = o.astype(o_ref.dtype)

    return decays, scores, recurrence, outputs


def _hgrn2_scan_kernel(layer, depth, xf_ref, xb_ref, mod_ref, nw_ref, wq_ref, wff_ref, wfb_ref, wi_ref,
                       lb_ref, s0_ref, of_ref, ob_ref, sfin_ref,
                       st_scr, h_scr, q_scr, k_scr, lf_scr, v_scr, hm_scr, qd_scr, ki_scr, kd_scr,
                       dec_scr, att_scr, u_scr, sb_scr):
    i = pl.program_id(1)

    @pl.when(i == 0)
    def _():
        st_scr[...] = s0_ref[...]

    lbs = _lower_bounds(lb_ref, layer, depth)
    stages = []
    for dirn, (x_ref, wf_ref, o_ref) in enumerate(((xf_ref, wff_ref, of_ref), (xb_ref, wfb_ref, ob_ref))):
        h_d = h_scr.at[dirn]
        h_d[...] = _ada_norm(x_ref[...], nw_ref[...], mod_ref[0:1, :], mod_ref[1:2, :]).astype(BF16)
        q = _dot(h_d[...], wq_ref[...])
        q_scr[dirn] = q * jax.nn.sigmoid(q)
        key, log2_f = _forget(_dot(h_d[...], wf_ref[...]), lbs[dirn])
        k_scr[dirn] = key
        lf_scr[dirn] = log2_f
        v_scr[dirn] = _dot(h_d[...], wi_ref[...]).astype(BF16)
        stages.append(_scan_tile(
            q_scr.at[dirn], k_scr.at[dirn], lf_scr.at[dirn], v_scr.at[dirn], hm_scr.at[dirn],
            qd_scr.at[dirn], ki_scr.at[dirn], kd_scr.at[dirn], dec_scr.at[dirn], att_scr.at[dirn],
            u_scr.at[dirn], sb_scr.at[dirn], st_scr.at[dirn], o_ref, rev=(dirn == 1)))
    for stage_pair in zip(*stages):
        for stage in stage_pair:
            stage()

    @pl.when(i == pl.num_programs(1) - 1)
    def _():
        sfin_ref[...] = st_scr[...]


def _hgrn2_scan(x, mod, mod_row, norm_w, wq, wff, wfb, wi, hg_lb, layer, s0, t):
    nb, n, d = x.shape
    hk = wq.shape[1]
    dk = hk // HG_HEADS
    nt = n // t
    depth = hg_lb.shape[0]
    mod_map = (lambda b, i: (b, 0, 0)) if mod_row is None else (lambda b, i: (mod_row, 0, 0))
    lb2 = hg_lb.reshape(depth * 2, hk)
    return pl.pallas_call(
        functools.partial(_hgrn2_scan_kernel, layer, depth),
        out_shape=(jax.ShapeDtypeStruct((nb, n, hk), BF16),
                   jax.ShapeDtypeStruct((nb, n, hk), BF16),
                   jax.ShapeDtypeStruct((nb, 2, hk, dk), F32)),
        grid=(nb, nt),
        in_specs=[
            pl.BlockSpec((None, t, d), lambda b, i: (b, i, 0)),
            pl.BlockSpec((None, t, d), lambda b, i: (b, nt - 1 - i, 0)),
            pl.BlockSpec((None, 6, d), mod_map),
            _const_spec((1, d)),
            _const_spec((d, hk)),
            _const_spec((d, hk)),
            _const_spec((d, hk)),
            _const_spec((d, hk)),
            _const_spec((depth * 2, hk)),
            pl.BlockSpec((None, 2, hk, dk), lambda b, i: (b, 0, 0, 0)),
        ],
        out_specs=(
            pl.BlockSpec((None, t, hk), lambda b, i: (b, i, 0)),
            pl.BlockSpec((None, t, hk), lambda b, i: (b, nt - 1 - i, 0)),
            pl.BlockSpec((None, 2, hk, dk), lambda b, i: (b, 0, 0, 0)),
        ),
        scratch_shapes=[
            pltpu.VMEM((2, hk, dk), F32),
            pltpu.VMEM((2, t, d), BF16),
            pltpu.VMEM((2, t, hk), F32),
            pltpu.VMEM((2, t, hk), F32),
            pltpu.VMEM((2, t, hk), F32),
            pltpu.VMEM((2, t, hk), BF16),
            pltpu.VMEM((2, t, 2 * hk), BF16),
            pltpu.VMEM((2, t, hk), BF16),
            pltpu.VMEM((2, t, hk), BF16),
            pltpu.VMEM((2, t, hk), BF16),
            pltpu.VMEM((2, max(t // SCAN_CHUNK, 8), hk), F32),
            pltpu.VMEM((2, t, hk), BF16),
            pltpu.VMEM((2, t // SCAN_CHUNK, hk, dk), F32),
            pltpu.VMEM((2, t // SCAN_CHUNK, hk, dk), BF16),
        ],
        compiler_params=pltpu.CompilerParams(
            dimension_semantics=("arbitrary", "arbitrary"),
            vmem_limit_bytes=V7X_VMEM_LIMIT_BYTES),
        name="hgrn2_scan",
    )(x, x, mod, norm_w.reshape(1, d), wq, wff, wfb, wi, lb2, s0)


def _hgrn2_ffn_kernel(final_norm, x_ref, of_ref, ob_ref, mod_ref, nwm_ref, wg_ref, hnw_ref, wo_ref,
                      nwf_ref, win_ref, wout_ref, *rest):
    if final_norm:
        fnw_ref, o_ref, h_scr, og_scr, g_scr = rest
    else:
        o_ref, h_scr, og_scr, g_scr = rest
    x = x_ref[...]
    hv = of_ref.shape[-1]
    dv = hv // HG_HEADS
    h_scr[...] = _ada_norm(x, nwm_ref[...], mod_ref[0:1, :], mod_ref[1:2, :]).astype(BF16)
    gate = _dot(h_scr[...], wg_ref[...])
    gate = gate * jax.nn.sigmoid(gate)
    for hh in range(HG_HEADS):
        cs = slice(hh * dv, (hh + 1) * dv)
        oh = of_ref[:, cs].astype(F32) + ob_ref[:, cs].astype(F32)
        on = _rms_norm(oh, hnw_ref[:, cs])
        og_scr[:, cs] = (on * gate[:, cs]).astype(BF16)
    y = _dot(og_scr[...], wo_ref[...])
    x = x + mod_ref[2:3, :] * y
    x = _swiglu_residual(x, mod_ref, nwf_ref, win_ref, wout_ref, h_scr, g_scr)
    if final_norm:
        x = _rms_norm(x, fnw_ref[...])
    o_ref[...] = x


def _hgrn2_ffn(x, o_f, o_b, mod, mod_row, norm_mix_w, wg, hg_norm_w, hg_w_out, norm_ffn_w, w_in_il, w_out,
               final_norm_w, tm):
    nb, n, d = x.shape
    hv = o_f.shape[-1]
    d_ff = w_out.shape[0]
    final_norm = final_norm_w is not None
    mod_map = (lambda b, i: (b, 0, 0)) if mod_row is None else (lambda b, i: (mod_row, 0, 0))
    in_specs = [
        pl.BlockSpec((None, tm, d), lambda b, i: (b, i, 0)),
        pl.BlockSpec((None, tm, hv), lambda b, i: (b, i, 0)),
        pl.BlockSpec((None, tm, hv), lambda b, i: (b, i, 0)),
        pl.BlockSpec((None, 6, d), mod_map),
        _const_spec((1, d)),
        _const_spec((d, hv)),
        _const_spec((1, hv)),
        _const_spec((hv, d)),
        _const_spec((1, d)),
        _const_spec((d, 2 * d_ff)),
        _const_spec((d_ff, d)),
    ]
    args = [x, o_f, o_b, mod, norm_mix_w.reshape(1, d), wg, hg_norm_w.reshape(1, hv), hg_w_out,
            norm_ffn_w.reshape(1, d), w_in_il, w_out]
    if final_norm:
        in_specs.append(_const_spec((1, d)))
        args.append(final_norm_w.reshape(1, d))
    return pl.pallas_call(
        functools.partial(_hgrn2_ffn_kernel, final_norm),
        out_shape=jax.ShapeDtypeStruct((nb, n, d), F32),
        grid=(nb, n // tm),
        in_specs=in_specs,
        out_specs=pl.BlockSpec((None, tm, d), lambda b, i: (b, i, 0)),
        scratch_shapes=[pltpu.VMEM((tm, d), BF16), pltpu.VMEM((tm, hv), BF16), pltpu.VMEM((tm, d_ff), BF16)],
        compiler_params=pltpu.CompilerParams(
            dimension_semantics=("arbitrary", "arbitrary"),
            vmem_limit_bytes=V7X_VMEM_LIMIT_BYTES),
        name="hgrn2_ffn",
    )(*args)


def _sincos(pos, dim):
    half = dim // 2
    omega = 1.0 / (POS_BASE ** (jnp.arange(half, dtype=F32) / half))
    ang = pos.astype(F32)[:, None] * omega[None, :]
    return jnp.concatenate([jnp.sin(ang), jnp.cos(ang)], axis=-1)


def _grid_pos_code(n, d):
    rows = n // GRID_W
    half = d // 2
    row_code = _sincos(jnp.arange(rows), half)
    col_code = _sincos(jnp.arange(GRID_W), half)
    code = jnp.concatenate([
        jnp.broadcast_to(row_code[:, None, :], (rows, GRID_W, half)),
        jnp.broadcast_to(col_code[None, :, :], (rows, GRID_W, half))], axis=-1)
    return code.reshape(rows * GRID_W, d)


def _pick_tile(n, target):
    t = min(n, target)
    while n % t:
        t //= 2
    return t


def kernel(x, c, ctx, c_ctx, ada_w, ada_b, norm_mix_w, norm_ffn_w, gm_w_in, gm_b_in, gm_ln_g, gm_ln_b,
           gm_w_s, gm_b_s, gm_w_out, hg_w_in, hg_lb, hg_norm_w, hg_w_out, ffn_w_in, ffn_w_out, final_norm_w):
    bsz, n, d = x.shape
    n_ctx = ctx.shape[1]
    depth = ada_w.shape[0]
    assert bsz + 1 <= MOD_ROWS
    hk = hg_lb.shape[-1]
    dk = hk // HG_HEADS

    cvec = jnp.zeros((MOD_ROWS, d), F32).at[:bsz].set(c).at[bsz].set(c_ctx)
    mod_all = _ada_mod(cvec, ada_w, ada_b).reshape(depth, MOD_ROWS, 6, d)
    ctx_row = bsz

    pos = _grid_pos_code(n, d)
    ctx_flat = ctx.reshape(1, bsz * n_ctx, d)

    tm_lat = _pick_tile(n, 256)
    tm_ctx = _pick_tile(n_ctx, 256)
    t_scan = _pick_tile(n, 256)

    for i in range(depth):
        last = i == depth - 1
        use_a = i % N_MIXERS == 0
        j = i // N_MIXERS
        mod = mod_all[i]
        ffn_in = _interleave_ffn_w_in(ffn_w_in[i]).astype(BF16)
        ffn_out = ffn_w_out[i].astype(BF16)

        if use_a:
            half = gm_ln_g.shape[-1]
            gd = half // GM_GROUPS
            gm = (norm_mix_w[i], gm_w_in[j].astype(BF16), gm_b_in[j], gm_ln_g[j], gm_ln_b[j],
                  gm_w_s[j].astype(BF16),
                  jnp.broadcast_to(gm_b_s[j][:, :, None], (GM_GROUPS, GM_CHUNK, gd)),
                  gm_w_out[j].astype(BF16))
            x = _gmlp_mixer(x, pos if i == 0 else None, mod, None, *gm, tm=tm_lat)
            x = _swiglu_ffn(x, mod, None, norm_ffn_w[i], ffn_in, ffn_out,
                            final_norm_w if last else None, tm=tm_lat)
            if not last:
                ctx_flat = _gmlp_mixer(ctx_flat, None, mod, ctx_row, *gm, tm=tm_ctx)
                ctx_flat = _swiglu_ffn(ctx_flat, mod, ctx_row, norm_ffn_w[i], ffn_in, ffn_out, None, tm=tm_ctx)
        else:
            if i == 0:
                x = x + pos
            w = hg_w_in[j].astype(BF16)
            wq, wff, wfb, wi, wg = (w[:, 0:hk], w[:, hk:2 * hk], w[:, 2 * hk:3 * hk], w[:, 3 * hk:4 * hk],
                                    w[:, 4 * hk:])
            scan = functools.partial(_hgrn2_scan, norm_w=norm_mix_w[i], wq=wq, wff=wff, wfb=wfb, wi=wi,
                                     hg_lb=hg_lb, layer=i)
            zero = jnp.zeros((bsz, 2, hk, dk), F32)
            ctx3 = ctx_flat.reshape(bsz, n_ctx, d)
            oc_f, oc_b, s_ctx = scan(ctx3, mod, ctx_row, s0=zero, t=n_ctx)
            o_f, o_b, _ = scan(x, mod, None, s0=s_ctx, t=t_scan)
            readout = functools.partial(_hgrn2_ffn, norm_mix_w=norm_mix_w[i], wg=wg, hg_norm_w=hg_norm_w[j],
                                        hg_w_out=hg_w_out[j].astype(BF16), norm_ffn_w=norm_ffn_w[i],
                                        w_in_il=ffn_in, w_out=ffn_out)
            x = readout(x, o_f, o_b, mod, None, final_norm_w=final_norm_w if last else None, tm=tm_lat)
            if not last:
                ctx3 = readout(ctx3, oc_f, oc_b, mod, ctx_row, final_norm_w=None, tm=tm_ctx)
                ctx_flat = ctx3.reshape(1, bsz * n_ctx, d)
    return x
```

```python
import functools

import jax
import jax.numpy as jnp
from jax import lax
from jax.experimental import pallas as pl
from jax.experimental.pallas import tpu as pltpu

NORM_EPS = 1e-6
POS_BASE = 10000.0
GRID_W = 64
N_MIXERS = 2
GM_CHUNK = 128
GM_GROUPS = 8
GROUPS_PER_DOT = 2
HG_HEADS = 8
SCAN_CHUNK = 64
MOD_ROWS = 16

V7X_VMEM_LIMIT_BYTES = 56 * 1024 * 1024
COL_BLOCK = 256

BF16 = jnp.bfloat16
F32 = jnp.float32


def _dot(a, b):
    return jnp.dot(a, b, preferred_element_type=F32)


def _dot_nt(a, b):
    return lax.dot_general(a, b, (((1,), (1,)), ((), ())), preferred_element_type=F32)


def _dot_tn(a, b):
    return lax.dot_general(a, b, (((0,), (0,)), ((), ())), preferred_element_type=F32)


def _ada_norm(x, w, shift, scale):
    ms = jnp.mean(x * x, axis=-1, keepdims=True)
    return (x * lax.rsqrt(ms + NORM_EPS) * w) * (1.0 + scale) + shift


def _gelu_exact(x):
    return 0.5 * x * (1.0 + lax.erf(x * (2.0 ** -0.5)))


def _const_spec(shape):
    return pl.BlockSpec(shape, lambda *_: (0,) * len(shape), pipeline_mode=pl.Buffered(1))


def _ada_mod_kernel(c_ref, w_ref, b_ref, o_ref):
    c = c_ref[...]
    s = c * jax.nn.sigmoid(c)
    o_ref[...] = jnp.dot(s, w_ref[...], preferred_element_type=F32,
                         precision=lax.Precision.HIGHEST) + b_ref[...]


def _ada_mod(cvec, ada_w, ada_b):
    depth, d, n6 = ada_w.shape
    tn = 1536
    return pl.pallas_call(
        _ada_mod_kernel,
        out_shape=jax.ShapeDtypeStruct((depth, MOD_ROWS, n6), F32),
        grid=(depth, n6 // tn),
        in_specs=[
            pl.BlockSpec((MOD_ROWS, d), lambda l, j: (0, 0)),
            pl.BlockSpec((None, d, tn), lambda l, j: (l, 0, j)),
            pl.BlockSpec((None, 1, tn), lambda l, j: (l, 0, j)),
        ],
        out_specs=pl.BlockSpec((None, MOD_ROWS, tn), lambda l, j: (l, 0, j)),
        compiler_params=pltpu.CompilerParams(
            dimension_semantics=("arbitrary", "arbitrary"),
            vmem_limit_bytes=V7X_VMEM_LIMIT_BYTES),
        name="ada_mod",
    )(cvec, ada_w, ada_b.reshape(depth, 1, n6))


def _gmlp_kernel(add_pos, *refs):
    if add_pos:
        x_ref, pos_ref = refs[0], refs[1]
        refs = refs[2:]
    else:
        x_ref, pos_ref = refs[0], None
        refs = refs[1:]
    (mod_ref, nw_ref, win_ref, bin_ref, lng_ref, lnb_ref, ws_ref, bs_ref, wout_ref,
     o_ref, h_scr, z_scr, vn_scr, p_scr) = refs
    tm, d = x_ref.shape
    half = lng_ref.shape[-1]
    gd = half // GM_GROUPS
    n_chunks = tm // GM_CHUNK

    x = x_ref[...]
    if add_pos:
        x = x + pos_ref[...]
    h_scr[...] = _ada_norm(x, nw_ref[...], mod_ref[0:1, :], mod_ref[1:2, :]).astype(BF16)

    def z_block(j):
        cs = slice(j * COL_BLOCK, (j + 1) * COL_BLOCK)
        z = _dot(h_scr[...], win_ref[:, cs]) + bin_ref[:, cs]
        z_scr[:, cs] = _gelu_exact(z)

    n_blk = half // COL_BLOCK
    for j in range(n_blk, 2 * n_blk):
        z_block(j)
    v = z_scr[:, half:]
    mu = jnp.mean(v, axis=-1, keepdims=True)
    vc = v - mu
    rstd = lax.rsqrt(jnp.mean(vc * vc, axis=-1, keepdims=True) + NORM_EPS)
    for j in range(n_blk):
        z_block(j)

    y = None
    for g in range(GM_GROUPS):
        ucs = slice(g * gd, (g + 1) * gd)
        vcs = slice(half + g * gd, half + (g + 1) * gd)
        vn_scr[:, ucs] = ((z_scr[:, vcs] - mu) * rstd * lng_ref[:, ucs] + lnb_ref[:, ucs]).astype(BF16)
        for c in range(n_chunks):
            rs = slice(c * GM_CHUNK, (c + 1) * GM_CHUNK)
            vs = _dot(ws_ref[g], vn_scr[rs, ucs]) + bs_ref[g]
            p_scr[rs, ucs] = (z_scr[rs, ucs] * vs).astype(BF16)
        if (g + 1) % GROUPS_PER_DOT == 0:
            ks = slice((g + 1 - GROUPS_PER_DOT) * gd, (g + 1) * gd)
            part = _dot(p_scr[:, ks], wout_ref[ks, :d])
            y = part if y is None else y + part
    o_ref[...] = x + mod_ref[2:3, :] * y


def _gmlp_mixer(x, pos, mod, mod_row, norm_w, w_in, b_in, ln_g, ln_b, w_s, b_s, w_out, tm):
    nb, n, d = x.shape
    half = ln_g.shape[-1]
    gd = half // GM_GROUPS
    add_pos = pos is not None
    mod_map = (lambda b, i: (b, 0, 0)) if mod_row is None else (lambda b, i: (mod_row, 0, 0))
    in_specs = [pl.BlockSpec((None, tm, d), lambda b, i: (b, i, 0))]
    args = [x]
    if add_pos:
        in_specs.append(pl.BlockSpec((tm, d), lambda b, i: (i, 0)))
        args.append(pos)
    in_specs += [
        pl.BlockSpec((None, 6, d), mod_map),
        _const_spec((1, d)),
        _const_spec(w_in.shape),
        _const_spec((1, 2 * half)),
        _const_spec((1, half)),
        _const_spec((1, half)),
        _const_spec((GM_GROUPS, GM_CHUNK, GM_CHUNK)),
        _const_spec((GM_GROUPS, GM_CHUNK, gd)),
        _const_spec(w_out.shape),
    ]
    args += [mod, norm_w.reshape(1, d), w_in, b_in.reshape(1, -1), ln_g.reshape(1, -1), ln_b.reshape(1, -1),
             w_s, b_s, w_out]
    return pl.pallas_call(
        functools.partial(_gmlp_kernel, add_pos),
        out_shape=jax.ShapeDtypeStruct((nb, n, d), F32),
        grid=(nb, n // tm),
        in_specs=in_specs,
        out_specs=pl.BlockSpec((None, tm, d), lambda b, i: (b, i, 0)),
        scratch_shapes=[pltpu.VMEM((tm, d), BF16), pltpu.VMEM((tm, 2 * half), F32),
                        pltpu.VMEM((tm, half), BF16), pltpu.VMEM((tm, half), BF16)],
        compiler_params=pltpu.CompilerParams(
            dimension_semantics=("arbitrary", "arbitrary"),
            vmem_limit_bytes=V7X_VMEM_LIMIT_BYTES),
        name="gmlp_mixer",
    )(*args)


def _swiglu_residual(x, mod_ref, nw_ref, win_ref, wout_ref, h_scr, g_scr):
    d = x.shape[-1]
    d_ff = g_scr.shape[-1]
    h_scr[...] = _ada_norm(x, nw_ref[...], mod_ref[3:4, :], mod_ref[4:5, :]).astype(BF16)
    for j in range(d_ff // COL_BLOCK):
        cs = slice(j * COL_BLOCK, (j + 1) * COL_BLOCK)
        a = _dot(h_scr[...], win_ref[:, cs])
        b = _dot(h_scr[...], win_ref[:, d_ff + j * COL_BLOCK:d_ff + (j + 1) * COL_BLOCK])
        g_scr[:, cs] = (a * jax.nn.sigmoid(a) * b).astype(BF16)
    y = _dot(g_scr[...], wout_ref[:, :d])
    return x + mod_ref[5:6, :] * y


def _rms_norm(x, w):
    ms = jnp.mean(x * x, axis=-1, keepdims=True)
    return x * lax.rsqrt(ms + NORM_EPS) * w


def _ffn_kernel(final_norm, x_ref, mod_ref, nw_ref, win_ref, wout_ref, *rest):
    if final_norm:
        fnw_ref, o_ref, h_scr, g_scr = rest
    else:
        o_ref, h_scr, g_scr = rest
    x = _swiglu_residual(x_ref[...], mod_ref, nw_ref, win_ref, wout_ref, h_scr, g_scr)
    if final_norm:
        x = _rms_norm(x, fnw_ref[...])
    o_ref[...] = x


def _swiglu_ffn(x, mod, mod_row, norm_w, w_in, w_out, final_norm_w, tm):
    nb, n, d = x.shape
    d_ff = w_out.shape[0]
    final_norm = final_norm_w is not None
    mod_map = (lambda b, i: (b, 0, 0)) if mod_row is None else (lambda b, i: (mod_row, 0, 0))
    in_specs = [
        pl.BlockSpec((None, tm, d), lambda b, i: (b, i, 0)),
        pl.BlockSpec((None, 6, d), mod_map),
        _const_spec((1, d)),
        _const_spec(w_in.shape),
        _const_spec(w_out.shape),
    ]
    args = [x, mod, norm_w.reshape(1, d), w_in, w_out]
    if final_norm:
        in_specs.append(_const_spec((1, d)))
        args.append(final_norm_w.reshape(1, d))
    return pl.pallas_call(
        functools.partial(_ffn_kernel, final_norm),
        out_shape=jax.ShapeDtypeStruct((nb, n, d), F32),
        grid=(nb, n // tm),
        in_specs=in_specs,
        out_specs=pl.BlockSpec((None, tm, d), lambda b, i: (b, i, 0)),
        scratch_shapes=[pltpu.VMEM((tm, d), BF16), pltpu.VMEM((tm, d_ff), BF16)],
        compiler_params=pltpu.CompilerParams(
            dimension_semantics=("arbitrary", "arbitrary"),
            vmem_limit_bytes=V7X_VMEM_LIMIT_BYTES),
        name="swiglu_ffn",
    )(*args)


def _lower_bounds(lb_ref, layer, depth):
    out = []
    for dirn in range(2):
        logits = [lb_ref[2 * l + dirn:2 * l + dirn + 1, :] for l in range(depth)]
        m = functools.reduce(jnp.maximum, logits)
        e = [jnp.exp(t - m) for t in logits]
        denom = functools.reduce(lambda a, b: a + b, e)
        num = jnp.zeros_like(m)
        for l in range(1, layer + 1):
            num = num + e[l]
        out.append(num / denom)
    return out


LOG2_E = 1.4426950408889634


def _forget(z, lb):
    t = jnp.exp(-jnp.abs(z))
    r = 1.0 / (1.0 + t)
    tr = t * r
    pos = z >= 0.0
    one_m = 1.0 - lb
    f = lb + one_m * jnp.where(pos, r, tr)
    key = one_m * jnp.where(pos, tr, r)
    return key, jnp.log(f) * LOG2_E


def _scan_tile(q_scr, k_scr, lf_scr, v_scr, hm_scr, lq_scr, ki_scr, kd_scr, dec_scr, u_scr, rv_scr,
               st_ref, o_ref, rev):
    t, hk = q_scr.shape
    dk = hk // HG_HEADS
    pitch = lq_scr.shape[-1] // HG_HEADS
    heads_per_block = COL_BLOCK // dk
    n_chunks = t // SCAN_CHUNK
    row = lax.broadcasted_iota(jnp.int32, (SCAN_CHUNK, SCAN_CHUNK), 0)
    col = lax.broadcasted_iota(jnp.int32, (SCAN_CHUNK, SCAN_CHUNK), 1)
    tri = (row <= col) if rev else (row >= col)
    tri_bf = tri.astype(BF16)
    last = 0 if rev else SCAN_CHUNK - 1
    order = range(n_chunks - 1, -1, -1) if rev else range(n_chunks)
    rows = [slice(c * SCAN_CHUNK, (c + 1) * SCAN_CHUNK) for c in range(n_chunks)]
    cols = [slice(hh * dk, (hh + 1) * dk) for hh in range(HG_HEADS)]

    def decays():
        for c in order:
            rs = rows[c]
            for j in range(hk // COL_BLOCK):
                cb = slice(j * COL_BLOCK, (j + 1) * COL_BLOCK)
                cb2 = slice(hk + j * COL_BLOCK, hk + (j + 1) * COL_BLOCK)
                lf = lf_scr[rs, cb]
                hi = lf.astype(BF16)
                hm_scr[rs, cb] = hi
                hm_scr[rs, cb2] = (lf - hi.astype(F32)).astype(BF16)
                b = _dot(tri_bf, hm_scr[rs, cb]) + _dot(tri_bf, hm_scr[rs, cb2])
                b_last = b[last:last + 1, :]
                kf = k_scr[rs, cb]
                qd = (q_scr[rs, cb] * jnp.exp2(b)).astype(BF16)
                for k in range(heads_per_block):
                    hh = j * heads_per_block + k
                    lq_scr[rs, hh * pitch:hh * pitch + dk] = qd[:, k * dk:(k + 1) * dk]
                ki_scr[rs, cb] = (kf * jnp.exp2(-b)).astype(BF16)
                kd_scr[rs, cb] = (kf * jnp.exp2(b_last - b)).astype(BF16)
                dec_scr[c:c + 1, cb] = jnp.exp2(b_last)

    def scores():
        for c in order:
            rs = rows[c]
            for hh, cs in enumerate(cols):
                att = _dot_nt(lq_scr[rs, hh * pitch:hh * pitch + dk], ki_scr[rs, cs])
                lq_scr[rs, hh * pitch + dk:hh * pitch + dk + SCAN_CHUNK] = jnp.where(tri, att, 0.0).astype(BF16)
                rv_scr[c, hh, dk:, :] = v_scr[rs, cs]
                u_scr[c, cs, :] = _dot_tn(v_scr[rs, cs], kd_scr[rs, cs])

    def recurrence():
        for c in order:
            for hh, cs in enumerate(cols):
                st = st_ref[cs, :]
                rv_scr[c, hh, :dk, :] = st.T.astype(BF16)
                st_ref[cs, :] = st * dec_scr[c:c + 1, cs] + u_scr[c, cs, :]

    def outputs():
        for c in order:
            rs = rows[c]
            for hh, cs in enumerate(cols):
                o = _dot(lq_scr[rs, hh * pitch:hh * pitch + dk + SCAN_CHUNK], rv_scr[c, hh])
                o_ref[rs, cs] = o.astype(o_ref.dtype)

    return decays, scores, recurrence, outputs


def _hgrn2_scan_kernel(layer, depth, xf_ref, xb_ref, mod_ref, nw_ref, w_ref,
                       lb_ref, s0_ref, of_ref, ob_ref, sfin_ref,
                       st_scr, h_scr, q_scr, k_scr, lf_scr, v_scr, hm_scr, lq_scr, ki_scr, kd_scr,
                       dec_scr, u_scr, rv_scr):
    i = pl.program_id(1)
    slot_in = lax.rem(i, 2)
    slot_out = 1 - slot_in

    @pl.when(i == 0)
    def _():
        st_scr[...] = s0_ref[...]
        q_scr[1] = jnp.zeros(q_scr.shape[1:], q_scr.dtype)
        k_scr[1] = jnp.zeros(k_scr.shape[1:], k_scr.dtype)
        lf_scr[1] = jnp.zeros(lf_scr.shape[1:], lf_scr.dtype)
        v_scr[1] = jnp.zeros(v_scr.shape[1:], v_scr.dtype)

    lbs = _lower_bounds(lb_ref, layer, depth)
    hk = lb_ref.shape[-1]
    def project(dirn, x_ref, j):
        h_d = h_scr.at[dirn]
        if j == 0:
            h_d[...] = _ada_norm(x_ref[...], nw_ref[...], mod_ref[0:1, :], mod_ref[1:2, :]).astype(BF16)
        q_in, k_in, lf_in, v_in = (r.at[slot_in, dirn] for r in (q_scr, k_scr, lf_scr, v_scr))
        cb = slice(j * COL_BLOCK, (j + 1) * COL_BLOCK)
        q = _dot(h_d[...], w_ref[:, cb])
        q_in[:, cb] = q * jax.nn.sigmoid(q)
        f_off = (1 + dirn) * hk + j * COL_BLOCK
        key, log2_f = _forget(_dot(h_d[...], w_ref[:, f_off:f_off + COL_BLOCK]), lbs[dirn][:, cb])
        k_in[:, cb] = key
        lf_in[:, cb] = log2_f
        i_off = 3 * hk + j * COL_BLOCK
        v_in[:, cb] = _dot(h_d[...], w_ref[:, i_off:i_off + COL_BLOCK]).astype(BF16)

    stages = []
    projections = []
    for dirn, (x_ref, o_ref) in enumerate(((xf_ref, of_ref), (xb_ref, ob_ref))):
        projections += [functools.partial(project, dirn, x_ref, j) for j in range(hk // COL_BLOCK)]
        stages.append(_scan_tile(
            q_scr.at[slot_out, dirn], k_scr.at[slot_out, dirn], lf_scr.at[slot_out, dirn],
            v_scr.at[slot_out, dirn], hm_scr.at[dirn], lq_scr.at[dirn], ki_scr.at[dirn], kd_scr.at[dirn],
            dec_scr.at[dirn], u_scr.at[dirn], rv_scr.at[dirn], st_scr.at[dirn], o_ref, rev=(dirn == 1)))
    for projection in projections:
        projection()
    for stage_pair in zip(*stages):
        for stage in stage_pair:
            stage()

    @pl.when(i == pl.num_programs(1) - 1)
    def _():
        sfin_ref[...] = st_scr[...]


def _hgrn2_scan(x, mod, mod_row, norm_w, w_in, hg_lb, layer, s0, t):
    nb, n, d = x.shape
    hk = hg_lb.shape[-1]
    dk = hk // HG_HEADS
    nt = n // t
    nc = t // SCAN_CHUNK
    depth = hg_lb.shape[0]
    mod_map = (lambda b, i: (b, 0, 0)) if mod_row is None else (lambda b, i: (mod_row, 0, 0))
    lb2 = hg_lb.reshape(depth * 2, hk)

    def tile_in(i):
        return jnp.minimum(i, nt - 1)

    def tile_out(i):
        return jnp.maximum(i - 1, 0)

    return pl.pallas_call(
        functools.partial(_hgrn2_scan_kernel, layer, depth),
        out_shape=(jax.ShapeDtypeStruct((nb, n, hk), BF16),
                   jax.ShapeDtypeStruct((nb, n, hk), BF16),
                   jax.ShapeDtypeStruct((nb, 2, hk, dk), F32)),
        grid=(nb, nt + 1),
        in_specs=[
            pl.BlockSpec((None, t, d), lambda b, i: (b, tile_in(i), 0)),
            pl.BlockSpec((None, t, d), lambda b, i: (b, nt - 1 - tile_in(i), 0)),
            pl.BlockSpec((None, 6, d), mod_map),
            _const_spec((1, d)),
            _const_spec(w_in.shape),
            _const_spec((depth * 2, hk)),
            pl.BlockSpec((None, 2, hk, dk), lambda b, i: (b, 0, 0, 0)),
        ],
        out_specs=(
            pl.BlockSpec((None, t, hk), lambda b, i: (b, tile_out(i), 0)),
            pl.BlockSpec((None, t, hk), lambda b, i: (b, nt - 1 - tile_out(i), 0)),
            pl.BlockSpec((None, 2, hk, dk), lambda b, i: (b, 0, 0, 0)),
        ),
        scratch_shapes=[
            pltpu.VMEM((2, hk, dk), F32),
            pltpu.VMEM((2, t, d), BF16),
            pltpu.VMEM((2, 2, t, hk), F32),
            pltpu.VMEM((2, 2, t, hk), F32),
            pltpu.VMEM((2, 2, t, hk), F32),
            pltpu.VMEM((2, 2, t, hk), BF16),
            pltpu.VMEM((2, t, 2 * hk), BF16),
            pltpu.VMEM((2, t, 2 * hk), BF16),
            pltpu.VMEM((2, t, hk), BF16),
            pltpu.VMEM((2, t, hk), BF16),
            pltpu.VMEM((2, max(nc, 8), hk), F32),
            pltpu.VMEM((2, nc, hk, dk), F32),
            pltpu.VMEM((2, nc, HG_HEADS, dk + SCAN_CHUNK, dk), BF16),
        ],
        compiler_params=pltpu.CompilerParams(
            dimension_semantics=("arbitrary", "arbitrary"),
            vmem_limit_bytes=V7X_VMEM_LIMIT_BYTES),
        name="hgrn2_scan",
    )(x, x, mod, norm_w.reshape(1, d), w_in, lb2, s0)


def _hgrn2_ffn_kernel(final_norm, g_off, x_ref, of_ref, ob_ref, mod_ref, nwm_ref, wg_ref, hnw_ref, wo_ref,
                      nwf_ref, win_ref, wout_ref, *rest):
    if final_norm:
        fnw_ref, o_ref, h_scr, og_scr, g_scr = rest
    else:
        o_ref, h_scr, og_scr, g_scr = rest
    x = x_ref[...]
    hv = of_ref.shape[-1]
    dv = hv // HG_HEADS
    d = x.shape[-1]
    h_scr[...] = _ada_norm(x, nwm_ref[...], mod_ref[0:1, :], mod_ref[1:2, :]).astype(BF16)
    for j in range(hv // COL_BLOCK):
        gate = _dot(h_scr[...], wg_ref[:, g_off + j * COL_BLOCK:g_off + (j + 1) * COL_BLOCK])
        gate = gate * jax.nn.sigmoid(gate)
        for hh in range(j * COL_BLOCK // dv, (j + 1) * COL_BLOCK // dv):
            cs = slice(hh * dv, (hh + 1) * dv)
            oh = of_ref[:, cs].astype(F32) + ob_ref[:, cs].astype(F32)
            on = _rms_norm(oh, hnw_ref[:, cs])
            og_scr[:, cs] = (on * gate[:, hh * dv - j * COL_BLOCK:(hh + 1) * dv - j * COL_BLOCK]).astype(BF16)
    y = _dot(og_scr[...], wo_ref[:, :d])
    x = x + mod_ref[2:3, :] * y
    x = _swiglu_residual(x, mod_ref, nwf_ref, win_ref, wout_ref, h_scr, g_scr)
    if final_norm:
        x = _rms_norm(x, fnw_ref[...])
    o_ref[...] = x


def _hgrn2_ffn(x, o_f, o_b, mod, mod_row, norm_mix_w, hg_w_in, g_off, hg_norm_w, hg_w_out, norm_ffn_w,
               w_in, w_out, final_norm_w, tm):
    nb, n, d = x.shape
    hv = o_f.shape[-1]
    d_ff = w_out.shape[0]
    final_norm = final_norm_w is not None
    mod_map = (lambda b, i: (b, 0, 0)) if mod_row is None else (lambda b, i: (mod_row, 0, 0))
    in_specs = [
        pl.BlockSpec((None, tm, d), lambda b, i: (b, i, 0)),
        pl.BlockSpec((None, tm, hv), lambda b, i: (b, i, 0)),
        pl.BlockSpec((None, tm, hv), lambda b, i: (b, i, 0)),
        pl.BlockSpec((None, 6, d), mod_map),
        _const_spec((1, d)),
        _const_spec(hg_w_in.shape),
        _const_spec((1, hv)),
        _const_spec(hg_w_out.shape),
        _const_spec((1, d)),
        _const_spec(w_in.shape),
        _const_spec(w_out.shape),
    ]
    args = [x, o_f, o_b, mod, norm_mix_w.reshape(1, d), hg_w_in, hg_norm_w.reshape(1, hv), hg_w_out,
            norm_ffn_w.reshape(1, d), w_in, w_out]
    if final_norm:
        in_specs.append(_const_spec((1, d)))
        args.append(final_norm_w.reshape(1, d))
    return pl.pallas_call(
        functools.partial(_hgrn2_ffn_kernel, final_norm, g_off),
        out_shape=jax.ShapeDtypeStruct((nb, n, d), F32),
        grid=(nb, n // tm),
        in_specs=in_specs,
        out_specs=pl.BlockSpec((None, tm, d), lambda b, i: (b, i, 0)),
        scratch_shapes=[pltpu.VMEM((tm, d), BF16), pltpu.VMEM((tm, hv), BF16), pltpu.VMEM((tm, d_ff), BF16)],
        compiler_params=pltpu.CompilerParams(
            dimension_semantics=("arbitrary", "arbitrary"),
            vmem_limit_bytes=V7X_VMEM_LIMIT_BYTES),
        name="hgrn2_ffn",
    )(*args)


def _sincos(pos, dim):
    half = dim // 2
    omega = 1.0 / (POS_BASE ** (jnp.arange(half, dtype=F32) / half))
    ang = pos.astype(F32)[:, None] * omega[None, :]
    return jnp.concatenate([jnp.sin(ang), jnp.cos(ang)], axis=-1)


def _grid_pos_code(n, d):
    rows = n // GRID_W
    half = d // 2
    row_code = _sincos(jnp.arange(rows), half)
    col_code = _sincos(jnp.arange(GRID_W), half)
    code = jnp.concatenate([
        jnp.broadcast_to(row_code[:, None, :], (rows, GRID_W, half)),
        jnp.broadcast_to(col_code[None, :, :], (rows, GRID_W, half))], axis=-1)
    return code.reshape(rows * GRID_W, d)


LANES = 128
STRIDED_LOAD_PERIOD = 8


def _mxu_weight(w):
    w = w.astype(BF16)
    if (w.shape[-1] // LANES) % STRIDED_LOAD_PERIOD == 0:
        w = jnp.pad(w, ((0, 0), (0, LANES)))
    return w


def _pick_tile(n, target):
    t = min(n, target)
    while n % t:
        t //= 2
    return t


def kernel(x, c, ctx, c_ctx, ada_w, ada_b, norm_mix_w, norm_ffn_w, gm_w_in, gm_b_in, gm_ln_g, gm_ln_b,
           gm_w_s, gm_b_s, gm_w_out, hg_w_in, hg_lb, hg_norm_w, hg_w_out, ffn_w_in, ffn_w_out, final_norm_w):
    bsz, n, d = x.shape
    n_ctx = ctx.shape[1]
    depth = ada_w.shape[0]
    assert bsz + 1 <= MOD_ROWS
    hk = hg_lb.shape[-1]
    dk = hk // HG_HEADS

    cvec = jnp.zeros((MOD_ROWS, d), F32).at[:bsz].set(c).at[bsz].set(c_ctx)
    mod_all = _ada_mod(cvec, ada_w, ada_b).reshape(depth, MOD_ROWS, 6, d)
    ctx_row = bsz

    pos = _grid_pos_code(n, d)
    ctx_flat = ctx.reshape(1, bsz * n_ctx, d)

    tm_lat = _pick_tile(n, 256)
    tm_ffn = _pick_tile(n, 512)
    tm_ctx = _pick_tile(n_ctx, 256)
    t_scan = _pick_tile(n, 256)

    for i in range(depth):
        last = i == depth - 1
        use_a = i % N_MIXERS == 0
        j = i // N_MIXERS
        mod = mod_all[i]
        ffn_in = _mxu_weight(ffn_w_in[i])
        ffn_out = _mxu_weight(ffn_w_out[i])

        if use_a:
            half = gm_ln_g.shape[-1]
            gd = half // GM_GROUPS
            gm = (norm_mix_w[i], _mxu_weight(gm_w_in[j]), gm_b_in[j], gm_ln_g[j], gm_ln_b[j],
                  gm_w_s[j].astype(BF16),
                  jnp.broadcast_to(gm_b_s[j][:, :, None], (GM_GROUPS, GM_CHUNK, gd)),
                  _mxu_weight(gm_w_out[j]))
            x = _gmlp_mixer(x, pos if i == 0 else None, mod, None, *gm, tm=tm_lat)
            x = _swiglu_ffn(x, mod, None, norm_ffn_w[i], ffn_in, ffn_out,
                            final_norm_w if last else None, tm=tm_ffn)
            if not last:
                ctx_flat = _gmlp_mixer(ctx_flat, None, mod, ctx_row, *gm, tm=tm_ctx)
                ctx_flat = _swiglu_ffn(ctx_flat, mod, ctx_row, norm_ffn_w[i], ffn_in, ffn_out, None, tm=tm_ctx)
        else:
            if i == 0:
                x = x + pos
            w = _mxu_weight(hg_w_in[j])
            scan = functools.partial(_hgrn2_scan, norm_w=norm_mix_w[i], w_in=w, hg_lb=hg_lb, layer=i)
            zero = jnp.zeros((bsz, 2, hk, dk), F32)
            ctx3 = ctx_flat.reshape(bsz, n_ctx, d)
            oc_f, oc_b, s_ctx = scan(ctx3, mod, ctx_row, s0=zero, t=n_ctx)
            o_f, o_b, _ = scan(x, mod, None, s0=s_ctx, t=t_scan)
            readout = functools.partial(_hgrn2_ffn, norm_mix_w=norm_mix_w[i], hg_w_in=w, g_off=4 * hk,
                                        hg_norm_w=hg_norm_w[j], hg_w_out=_mxu_weight(hg_w_out[j]),
                                        norm_ffn_w=norm_ffn_w[i], w_in=ffn_in, w_out=ffn_out)
            x = readout(x, o_f, o_b, mod, None, final_norm_w=final_norm_w if last else None, tm=tm_ffn)
            if not last:
                ctx3 = readout(ctx3, oc_f, oc_b, mod, ctx_row, final_norm_w=None, tm=tm_ctx)
                ctx_flat = ctx3.reshape(1, bsz * n_ctx, d)
    return x
```

```python
import functools

import jax
import jax.numpy as jnp
from jax import lax
from jax.experimental import pallas as pl
from jax.experimental.pallas import tpu as pltpu

NORM_EPS = 1e-6
POS_BASE = 10000.0
GRID_W = 64
N_MIXERS = 2
GM_CHUNK = 128
GM_GROUPS = 8
GROUPS_PER_DOT = 2
HG_HEADS = 8
SCAN_CHUNK = 64
MOD_ROWS = 16

V7X_VMEM_LIMIT_BYTES = 56 * 1024 * 1024
COL_BLOCK = 256
LANES = 128
STRIDED_LOAD_PERIOD = 8

BF16 = jnp.bfloat16
F32 = jnp.float32


def _dot(a, b):
    return jnp.dot(a, b, preferred_element_type=F32)


def _dot_nt(a, b):
    return lax.dot_general(a, b, (((1,), (1,)), ((), ())), preferred_element_type=F32)


def _dot_tn(a, b):
    return lax.dot_general(a, b, (((0,), (0,)), ((), ())), preferred_element_type=F32)


def _ada_norm(x, w, shift, scale):
    ms = jnp.mean(x * x, axis=-1, keepdims=True)
    return (x * lax.rsqrt(ms + NORM_EPS) * w) * (1.0 + scale) + shift


def _gelu_exact(x):
    return 0.5 * x * (1.0 + lax.erf(x * (2.0 ** -0.5)))


def _const_spec(shape):
    return pl.BlockSpec(shape, lambda *_: (0,) * len(shape), pipeline_mode=pl.Buffered(1))


def _ada_mod_kernel(c_ref, w_ref, b_ref, o_ref):
    c = c_ref[...]
    s = c * jax.nn.sigmoid(c)
    o_ref[...] = jnp.dot(s, w_ref[...], preferred_element_type=F32,
                         precision=lax.Precision.HIGHEST) + b_ref[...]


def _ada_mod(cvec, ada_w, ada_b):
    depth, d, n6 = ada_w.shape
    tn = 1536
    return pl.pallas_call(
        _ada_mod_kernel,
        out_shape=jax.ShapeDtypeStruct((depth, MOD_ROWS, n6), F32),
        grid=(depth, n6 // tn),
        in_specs=[
            pl.BlockSpec((MOD_ROWS, d), lambda l, j: (0, 0)),
            pl.BlockSpec((None, d, tn), lambda l, j: (l, 0, j)),
            pl.BlockSpec((None, 1, tn), lambda l, j: (l, 0, j)),
        ],
        out_specs=pl.BlockSpec((None, MOD_ROWS, tn), lambda l, j: (l, 0, j)),
        compiler_params=pltpu.CompilerParams(
            dimension_semantics=("arbitrary", "arbitrary"),
            vmem_limit_bytes=V7X_VMEM_LIMIT_BYTES),
        name="ada_mod",
    )(cvec, ada_w, ada_b.reshape(depth, 1, n6))


def _gmlp_kernel(add_pos, *refs):
    if add_pos:
        x_ref, pos_ref = refs[0], refs[1]
        refs = refs[2:]
    else:
        x_ref, pos_ref = refs[0], None
        refs = refs[1:]
    (mod_ref, nw_ref, win_ref, bin_ref, lng_ref, lnb_ref, ws_ref, bs_ref, wout_ref,
     o_ref, h_scr, z_scr, vn_scr, p_scr) = refs
    tm, d = x_ref.shape
    half = lng_ref.shape[-1]
    gd = half // GM_GROUPS
    n_chunks = tm // GM_CHUNK

    x = x_ref[...]
    if add_pos:
        x = x + pos_ref[...]
    h_scr[...] = _ada_norm(x, nw_ref[...], mod_ref[0:1, :], mod_ref[1:2, :]).astype(BF16)

    def z_block(j):
        cs = slice(j * COL_BLOCK, (j + 1) * COL_BLOCK)
        z = _dot(h_scr[...], win_ref[:, cs]) + bin_ref[:, cs]
        z_scr[:, cs] = _gelu_exact(z)

    n_blk = half // COL_BLOCK
    for j in range(n_blk, 2 * n_blk):
        z_block(j)
    v = z_scr[:, half:2 * half]
    mu = jnp.mean(v, axis=-1, keepdims=True)
    vc = v - mu
    rstd = lax.rsqrt(jnp.mean(vc * vc, axis=-1, keepdims=True) + NORM_EPS)
    for j in range(n_blk):
        z_block(j)

    y = None
    for g in range(GM_GROUPS):
        ucs = slice(g * gd, (g + 1) * gd)
        vcs = slice(half + g * gd, half + (g + 1) * gd)
        vn_scr[:, ucs] = ((z_scr[:, vcs] - mu) * rstd * lng_ref[:, ucs] + lnb_ref[:, ucs]).astype(BF16)
        for c in range(n_chunks):
            rs = slice(c * GM_CHUNK, (c + 1) * GM_CHUNK)
            vs = _dot(ws_ref[g], vn_scr[rs, ucs]) + bs_ref[g]
            p_scr[rs, ucs] = (z_scr[rs, ucs] * vs).astype(BF16)
        if (g + 1) % GROUPS_PER_DOT == 0:
            ks = slice((g + 1 - GROUPS_PER_DOT) * gd, (g + 1) * gd)
            part = _dot(p_scr[:, ks], wout_ref[ks, :d])
            y = part if y is None else y + part
    o_ref[...] = x + mod_ref[2:3, :] * y


def _gmlp_mixer(x, pos, mod, mod_row, norm_w, w_in, b_in, ln_g, ln_b, w_s, b_s, w_out, tm):
    nb, n, d = x.shape
    half = ln_g.shape[-1]
    gd = half // GM_GROUPS
    add_pos = pos is not None
    mod_map = (lambda b, i: (b, 0, 0)) if mod_row is None else (lambda b, i: (mod_row, 0, 0))
    in_specs = [pl.BlockSpec((None, tm, d), lambda b, i: (b, i, 0))]
    args = [x]
    if add_pos:
        in_specs.append(pl.BlockSpec((tm, d), lambda b, i: (i, 0)))
        args.append(pos)
    in_specs += [
        pl.BlockSpec((None, 6, d), mod_map),
        _const_spec((1, d)),
        _const_spec(w_in.shape),
        _const_spec((1, 2 * half)),
        _const_spec((1, half)),
        _const_spec((1, half)),
        _const_spec((GM_GROUPS, GM_CHUNK, GM_CHUNK)),
        _const_spec((GM_GROUPS, GM_CHUNK, gd)),
        _const_spec(w_out.shape),
    ]
    args += [mod, norm_w.reshape(1, d), w_in, b_in.reshape(1, -1), ln_g.reshape(1, -1), ln_b.reshape(1, -1),
             w_s, b_s, w_out]
    return pl.pallas_call(
        functools.partial(_gmlp_kernel, add_pos),
        out_shape=jax.ShapeDtypeStruct((nb, n, d), F32),
        grid=(nb, n // tm),
        in_specs=in_specs,
        out_specs=pl.BlockSpec((None, tm, d), lambda b, i: (b, i, 0)),
        scratch_shapes=[pltpu.VMEM((tm, d), BF16), pltpu.VMEM((tm, 2 * half + LANES), F32),
                        pltpu.VMEM((tm, half + LANES), BF16), pltpu.VMEM((tm, half + LANES), BF16)],
        compiler_params=pltpu.CompilerParams(
            dimension_semantics=("arbitrary", "arbitrary"),
            vmem_limit_bytes=V7X_VMEM_LIMIT_BYTES),
        name="gmlp_mixer",
    )(*args)


def _swiglu_residual(x, mod_ref, nw_ref, win_ref, wout_ref, h_scr, g_scr):
    d = x.shape[-1]
    d_ff = g_scr.shape[-1]
    h_scr[...] = _ada_norm(x, nw_ref[...], mod_ref[3:4, :], mod_ref[4:5, :]).astype(BF16)
    for j in range(d_ff // COL_BLOCK):
        cs = slice(j * COL_BLOCK, (j + 1) * COL_BLOCK)
        a = _dot(h_scr[...], win_ref[:, cs])
        b = _dot(h_scr[...], win_ref[:, d_ff + j * COL_BLOCK:d_ff + (j + 1) * COL_BLOCK])
        g_scr[:, cs] = (a * jax.nn.sigmoid(a) * b).astype(BF16)
    y = _dot(g_scr[...], wout_ref[:, :d])
    return x + mod_ref[5:6, :] * y


def _rms_norm(x, w):
    ms = jnp.mean(x * x, axis=-1, keepdims=True)
    return x * lax.rsqrt(ms + NORM_EPS) * w


def _ffn_kernel(final_norm, x_ref, mod_ref, nw_ref, win_ref, wout_ref, *rest):
    if final_norm:
        fnw_ref, o_ref, h_scr, g_scr = rest
    else:
        o_ref, h_scr, g_scr = rest
    x = _swiglu_residual(x_ref[...], mod_ref, nw_ref, win_ref, wout_ref, h_scr, g_scr)
    if final_norm:
        x = _rms_norm(x, fnw_ref[...])
    o_ref[...] = x


def _swiglu_ffn(x, mod, mod_row, norm_w, w_in, w_out, final_norm_w, tm):
    nb, n, d = x.shape
    d_ff = w_out.shape[0]
    final_norm = final_norm_w is not None
    mod_map = (lambda b, i: (b, 0, 0)) if mod_row is None else (lambda b, i: (mod_row, 0, 0))
    in_specs = [
        pl.BlockSpec((None, tm, d), lambda b, i: (b, i, 0)),
        pl.BlockSpec((None, 6, d), mod_map),
        _const_spec((1, d)),
        _const_spec(w_in.shape),
        _const_spec(w_out.shape),
    ]
    args = [x, mod, norm_w.reshape(1, d), w_in, w_out]
    if final_norm:
        in_specs.append(_const_spec((1, d)))
        args.append(final_norm_w.reshape(1, d))
    return pl.pallas_call(
        functools.partial(_ffn_kernel, final_norm),
        out_shape=jax.ShapeDtypeStruct((nb, n, d), F32),
        grid=(nb, n // tm),
        in_specs=in_specs,
        out_specs=pl.BlockSpec((None, tm, d), lambda b, i: (b, i, 0)),
        scratch_shapes=[pltpu.VMEM((tm, d), BF16), pltpu.VMEM((tm, d_ff), BF16)],
        compiler_params=pltpu.CompilerParams(
            dimension_semantics=("arbitrary", "arbitrary"),
            vmem_limit_bytes=V7X_VMEM_LIMIT_BYTES),
        name="swiglu_ffn",
    )(*args)


def _lower_bounds(lb_ref, layer, depth):
    out = []
    for dirn in range(2):
        logits = [lb_ref[2 * l + dirn:2 * l + dirn + 1, :] for l in range(depth)]
        m = functools.reduce(jnp.maximum, logits)
        e = [jnp.exp(t - m) for t in logits]
        denom = functools.reduce(lambda a, b: a + b, e)
        num = jnp.zeros_like(m)
        for l in range(1, layer + 1):
            num = num + e[l]
        out.append(num / denom)
    return out


LOG2_E = 1.4426950408889634


def _forget(z, lb):
    t = jnp.exp(-jnp.abs(z))
    r = 1.0 / (1.0 + t)
    tr = t * r
    pos = z >= 0.0
    one_m = 1.0 - lb
    f = lb + one_m * jnp.where(pos, r, tr)
    key = one_m * jnp.where(pos, tr, r)
    return key, jnp.log(f) * LOG2_E


def _scan_tile(q_scr, k_scr, lf_scr, v_scr, hm_scr, lq_scr, ki_scr, kd_scr, dec_scr, u_scr, rv_scr,
               st_ref, o_ref, rev):
    t = q_scr.shape[0]
    hk, dk = st_ref.shape
    pitch = 2 * dk
    heads_per_block = COL_BLOCK // dk
    n_chunks = t // SCAN_CHUNK
    row = lax.broadcasted_iota(jnp.int32, (SCAN_CHUNK, SCAN_CHUNK), 0)
    col = lax.broadcasted_iota(jnp.int32, (SCAN_CHUNK, SCAN_CHUNK), 1)
    tri = (row <= col) if rev else (row >= col)
    tri_bf = tri.astype(BF16)
    last = 0 if rev else SCAN_CHUNK - 1
    order = range(n_chunks - 1, -1, -1) if rev else range(n_chunks)
    rows = [slice(c * SCAN_CHUNK, (c + 1) * SCAN_CHUNK) for c in range(n_chunks)]
    cols = [slice(hh * dk, (hh + 1) * dk) for hh in range(HG_HEADS)]

    def decays():
        for c in order:
            rs = rows[c]
            for j in range(hk // COL_BLOCK):
                cb = slice(j * COL_BLOCK, (j + 1) * COL_BLOCK)
                cb2 = slice(hk + j * COL_BLOCK, hk + (j + 1) * COL_BLOCK)
                lf = lf_scr[rs, cb]
                hi = lf.astype(BF16)
                hm_scr[rs, cb] = hi
                hm_scr[rs, cb2] = (lf - hi.astype(F32)).astype(BF16)
                b = _dot(tri_bf, hm_scr[rs, cb]) + _dot(tri_bf, hm_scr[rs, cb2])
                b_last = b[last:last + 1, :]
                kf = k_scr[rs, cb]
                qd = (q_scr[rs, cb] * jnp.exp2(b)).astype(BF16)
                for k in range(heads_per_block):
                    hh = j * heads_per_block + k
                    lq_scr[rs, hh * pitch:hh * pitch + dk] = qd[:, k * dk:(k + 1) * dk]
                ki_scr[rs, cb] = (kf * jnp.exp2(-b)).astype(BF16)
                kd_scr[rs, cb] = (kf * jnp.exp2(b_last - b)).astype(BF16)
                dec_scr[c:c + 1, cb] = jnp.exp2(b_last)

    def scores():
        for c in order:
            rs = rows[c]
            for hh, cs in enumerate(cols):
                att = _dot_nt(lq_scr[rs, hh * pitch:hh * pitch + dk], ki_scr[rs, cs])
                lq_scr[rs, hh * pitch + dk:hh * pitch + dk + SCAN_CHUNK] = jnp.where(tri, att, 0.0).astype(BF16)
                rv_scr[c, hh, dk:, :] = v_scr[rs, cs]
                u_scr[c, cs, :] = _dot_tn(v_scr[rs, cs], kd_scr[rs, cs])

    def recurrence():
        for c in order:
            for hh, cs in enumerate(cols):
                st = st_ref[cs, :]
                rv_scr[c, hh, :dk, :] = st.T.astype(BF16)
                st_ref[cs, :] = st * dec_scr[c:c + 1, cs] + u_scr[c, cs, :]

    def outputs():
        for c in order:
            rs = rows[c]
            for hh, cs in enumerate(cols):
                o = _dot(lq_scr[rs, hh * pitch:hh * pitch + dk + SCAN_CHUNK], rv_scr[c, hh])
                o_ref[rs, cs] = o.astype(o_ref.dtype)

    return decays, scores, recurrence, outputs


def _hgrn2_scan_kernel(layer, depth, pipelined, xf_ref, xb_ref, mod_ref, nw_ref, w_ref,
                       lb_ref, s0_ref, of_ref, ob_ref, sfin_ref,
                       st_scr, h_scr, q_scr, k_scr, lf_scr, v_scr, hm_scr, lq_scr, ki_scr, kd_scr,
                       dec_scr, u_scr, rv_scr):
    i = pl.program_id(1)
    if pipelined:
        slot_in = lax.rem(i, 2)
        slot_out = 1 - slot_in
    else:
        slot_in = slot_out = 0

    @pl.when(i == 0)
    def _():
        st_scr[...] = s0_ref[...]
        if pipelined:
            q_scr[1] = jnp.zeros(q_scr.shape[1:], q_scr.dtype)
            k_scr[1] = jnp.zeros(k_scr.shape[1:], k_scr.dtype)
            lf_scr[1] = jnp.zeros(lf_scr.shape[1:], lf_scr.dtype)
            v_scr[1] = jnp.zeros(v_scr.shape[1:], v_scr.dtype)

    lbs = _lower_bounds(lb_ref, layer, depth)
    hk = lb_ref.shape[-1]
    def project(dirn, x_ref, j):
        h_d = h_scr.at[dirn]
        if j == 0:
            h_d[...] = _ada_norm(x_ref[...], nw_ref[...], mod_ref[0:1, :], mod_ref[1:2, :]).astype(BF16)
        q_in, k_in, lf_in, v_in = (r.at[slot_in, dirn] for r in (q_scr, k_scr, lf_scr, v_scr))
        cb = slice(j * COL_BLOCK, (j + 1) * COL_BLOCK)
        q = _dot(h_d[...], w_ref[:, cb])
        q_in[:, cb] = q * jax.nn.sigmoid(q)
        f_off = (1 + dirn) * hk + j * COL_BLOCK
        key, log2_f = _forget(_dot(h_d[...], w_ref[:, f_off:f_off + COL_BLOCK]), lbs[dirn][:, cb])
        k_in[:, cb] = key
        lf_in[:, cb] = log2_f
        i_off = 3 * hk + j * COL_BLOCK
        v_in[:, cb] = _dot(h_d[...], w_ref[:, i_off:i_off + COL_BLOCK]).astype(BF16)

    stages = []
    projections = []
    for dirn, (x_ref, o_ref) in enumerate(((xf_ref, of_ref), (xb_ref, ob_ref))):
        projections += [functools.partial(project, dirn, x_ref, j) for j in range(hk // COL_BLOCK)]
        stages.append(_scan_tile(
            q_scr.at[slot_out, dirn], k_scr.at[slot_out, dirn], lf_scr.at[slot_out, dirn],
            v_scr.at[slot_out, dirn], hm_scr.at[dirn], lq_scr.at[dirn], ki_scr.at[dirn], kd_scr.at[dirn],
            dec_scr.at[dirn], u_scr.at[dirn], rv_scr.at[dirn], st_scr.at[dirn], o_ref, rev=(dirn == 1)))
    for projection in projections:
        projection()
    for stage_pair in zip(*stages):
        for stage in stage_pair:
            stage()

    @pl.when(i == pl.num_programs(1) - 1)
    def _():
        sfin_ref[...] = st_scr[...]


def _hgrn2_scan(x, mod, mod_row, norm_w, w_in, hg_lb, layer, s0, t):
    nb, n, d = x.shape
    hk = hg_lb.shape[-1]
    dk = hk // HG_HEADS
    nt = n // t
    nc = t // SCAN_CHUNK
    depth = hg_lb.shape[0]
    mod_map = (lambda b, i: (b, 0, 0)) if mod_row is None else (lambda b, i: (mod_row, 0, 0))
    lb2 = hg_lb.reshape(depth * 2, hk)

    pipelined = nt > 1
    n_sets = 2 if pipelined else 1

    def tile_in(i):
        return jnp.minimum(i, nt - 1) if pipelined else i

    def tile_out(i):
        return jnp.maximum(i - 1, 0) if pipelined else i

    return pl.pallas_call(
        functools.partial(_hgrn2_scan_kernel, layer, depth, pipelined),
        out_shape=(jax.ShapeDtypeStruct((nb, n, hk), BF16),
                   jax.ShapeDtypeStruct((nb, n, hk), BF16),
                   jax.ShapeDtypeStruct((nb, 2, hk, dk), F32)),
        grid=(nb, nt + 1 if pipelined else nt),
        in_specs=[
            pl.BlockSpec((None, t, d), lambda b, i: (b, tile_in(i), 0)),
            pl.BlockSpec((None, t, d), lambda b, i: (b, nt - 1 - tile_in(i), 0)),
            pl.BlockSpec((None, 6, d), mod_map),
            _const_spec((1, d)),
            _const_spec(w_in.shape),
            _const_spec((depth * 2, hk)),
            pl.BlockSpec((None, 2, hk, dk), lambda b, i: (b, 0, 0, 0)),
        ],
        out_specs=(
            pl.BlockSpec((None, t, hk), lambda b, i: (b, tile_out(i), 0)),
            pl.BlockSpec((None, t, hk), lambda b, i: (b, nt - 1 - tile_out(i), 0)),
            pl.BlockSpec((None, 2, hk, dk), lambda b, i: (b, 0, 0, 0)),
        ),
        scratch_shapes=[
            pltpu.VMEM((2, hk, dk), F32),
            pltpu.VMEM((2, t, d), BF16),
            pltpu.VMEM((n_sets, 2, t, hk + LANES), F32),
            pltpu.VMEM((n_sets, 2, t, hk + LANES), F32),
            pltpu.VMEM((n_sets, 2, t, hk + LANES), F32),
            pltpu.VMEM((n_sets, 2, t, hk + LANES), BF16),
            pltpu.VMEM((2, t, 2 * hk + LANES), BF16),
            pltpu.VMEM((2, t, 2 * hk + LANES), BF16),
            pltpu.VMEM((2, t, hk + LANES), BF16),
            pltpu.VMEM((2, t, hk + LANES), BF16),
            pltpu.VMEM((2, max(nc, 8), hk), F32),
            pltpu.VMEM((2, nc, hk, dk), F32),
            pltpu.VMEM((2, nc, HG_HEADS, dk + SCAN_CHUNK, dk), BF16),
        ],
        compiler_params=pltpu.CompilerParams(
            dimension_semantics=("arbitrary", "arbitrary"),
            vmem_limit_bytes=V7X_VMEM_LIMIT_BYTES),
        name="hgrn2_scan",
    )(x, x, mod, norm_w.reshape(1, d), w_in, lb2, s0)


def _hgrn2_ffn_kernel(final_norm, g_off, x_ref, of_ref, ob_ref, mod_ref, nwm_ref, wg_ref, hnw_ref, wo_ref,
                      nwf_ref, win_ref, wout_ref, *rest):
    if final_norm:
        fnw_ref, o_ref, h_scr, og_scr, g_scr = rest
    else:
        o_ref, h_scr, og_scr, g_scr = rest
    x = x_ref[...]
    hv = of_ref.shape[-1]
    dv = hv // HG_HEADS
    d = x.shape[-1]
    h_scr[...] = _ada_norm(x, nwm_ref[...], mod_ref[0:1, :], mod_ref[1:2, :]).astype(BF16)
    for j in range(hv // COL_BLOCK):
        gate = _dot(h_scr[...], wg_ref[:, g_off + j * COL_BLOCK:g_off + (j + 1) * COL_BLOCK])
        gate = gate * jax.nn.sigmoid(gate)
        for hh in range(j * COL_BLOCK // dv, (j + 1) * COL_BLOCK // dv):
            cs = slice(hh * dv, (hh + 1) * dv)
            oh = of_ref[:, cs].astype(F32) + ob_ref[:, cs].astype(F32)
            on = _rms_norm(oh, hnw_ref[:, cs])
            og_scr[:, cs] = (on * gate[:, hh * dv - j * COL_BLOCK:(hh + 1) * dv - j * COL_BLOCK]).astype(BF16)
    y = _dot(og_scr[...], wo_ref[:, :d])
    x = x + mod_ref[2:3, :] * y
    x = _swiglu_residual(x, mod_ref, nwf_ref, win_ref, wout_ref, h_scr, g_scr)
    if final_norm:
        x = _rms_norm(x, fnw_ref[...])
    o_ref[...] = x


def _hgrn2_ffn(x, o_f, o_b, mod, mod_row, norm_mix_w, hg_w_in, g_off, hg_norm_w, hg_w_out, norm_ffn_w,
               w_in, w_out, final_norm_w, tm):
    nb, n, d = x.shape
    hv = o_f.shape[-1]
    d_ff = w_out.shape[0]
    final_norm = final_norm_w is not None
    mod_map = (lambda b, i: (b, 0, 0)) if mod_row is None else (lambda b, i: (mod_row, 0, 0))
    in_specs = [
        pl.BlockSpec((None, tm, d), lambda b, i: (b, i, 0)),
        pl.BlockSpec((None, tm, hv), lambda b, i: (b, i, 0)),
        pl.BlockSpec((None, tm, hv), lambda b, i: (b, i, 0)),
        pl.BlockSpec((None, 6, d), mod_map),
        _const_spec((1, d)),
        _const_spec(hg_w_in.shape),
        _const_spec((1, hv)),
        _const_spec(hg_w_out.shape),
        _const_spec((1, d)),
        _const_spec(w_in.shape),
        _const_spec(w_out.shape),
    ]
    args = [x, o_f, o_b, mod, norm_mix_w.reshape(1, d), hg_w_in, hg_norm_w.reshape(1, hv), hg_w_out,
            norm_ffn_w.reshape(1, d), w_in, w_out]
    if final_norm:
        in_specs.append(_const_spec((1, d)))
        args.append(final_norm_w.reshape(1, d))
    return pl.pallas_call(
        functools.partial(_hgrn2_ffn_kernel, final_norm, g_off),
        out_shape=jax.ShapeDtypeStruct((nb, n, d), F32),
        grid=(nb, n // tm),
        in_specs=in_specs,
        out_specs=pl.BlockSpec((None, tm, d), lambda b, i: (b, i, 0)),
        scratch_shapes=[pltpu.VMEM((tm, d), BF16), pltpu.VMEM((tm, hv), BF16), pltpu.VMEM((tm, d_ff), BF16)],
        compiler_params=pltpu.CompilerParams(
            dimension_semantics=("arbitrary", "arbitrary"),
            vmem_limit_bytes=V7X_VMEM_LIMIT_BYTES),
        name="hgrn2_ffn",
    )(*args)


def _sincos(pos, dim):
    half = dim // 2
    omega = 1.0 / (POS_BASE ** (jnp.arange(half, dtype=F32) / half))
    ang = pos.astype(F32)[:, None] * omega[None, :]
    return jnp.concatenate([jnp.sin(ang), jnp.cos(ang)], axis=-1)


def _grid_pos_code(n, d):
    rows = n // GRID_W
    half = d // 2
    row_code = _sincos(jnp.arange(rows), half)
    col_code = _sincos(jnp.arange(GRID_W), half)
    code = jnp.concatenate([
        jnp.broadcast_to(row_code[:, None, :], (rows, GRID_W, half)),
        jnp.broadcast_to(col_code[None, :, :], (rows, GRID_W, half))], axis=-1)
    return code.reshape(rows * GRID_W, d)


def _mxu_weight(w):
    w = w.astype(BF16)
    if (w.shape[-1] // LANES) % STRIDED_LOAD_PERIOD == 0:
        w = jnp.pad(w, ((0, 0), (0, LANES)))
    return w


def _pick_tile(n, target):
    t = min(n, target)
    while n % t:
        t //= 2
    return t


def kernel(x, c, ctx, c_ctx, ada_w, ada_b, norm_mix_w, norm_ffn_w, gm_w_in, gm_b_in, gm_ln_g, gm_ln_b,
           gm_w_s, gm_b_s, gm_w_out, hg_w_in, hg_lb, hg_norm_w, hg_w_out, ffn_w_in, ffn_w_out, final_norm_w):
    bsz, n, d = x.shape
    n_ctx = ctx.shape[1]
    depth = ada_w.shape[0]
    assert bsz + 1 <= MOD_ROWS
    hk = hg_lb.shape[-1]
    dk = hk // HG_HEADS

    cvec = jnp.zeros((MOD_ROWS, d), F32).at[:bsz].set(c).at[bsz].set(c_ctx)
    mod_all = _ada_mod(cvec, ada_w, ada_b).reshape(depth, MOD_ROWS, 6, d)
    ctx_row = bsz

    pos = _grid_pos_code(n, d)
    ctx_flat = ctx.reshape(1, bsz * n_ctx, d)

    tm_lat = _pick_tile(n, 256)
    tm_ffn = _pick_tile(n, 512)
    tm_ctx = _pick_tile(n_ctx, 256)
    t_scan = _pick_tile(n, 256)

    for i in range(depth):
        last = i == depth - 1
        use_a = i % N_MIXERS == 0
        j = i // N_MIXERS
        mod = mod_all[i]
        ffn_in = _mxu_weight(ffn_w_in[i])
        ffn_out = _mxu_weight(ffn_w_out[i])

        if use_a:
            half = gm_ln_g.shape[-1]
            gd = half // GM_GROUPS
            gm = (norm_mix_w[i], _mxu_weight(gm_w_in[j]), gm_b_in[j], gm_ln_g[j], gm_ln_b[j],
                  gm_w_s[j].astype(BF16),
                  jnp.broadcast_to(gm_b_s[j][:, :, None], (GM_GROUPS, GM_CHUNK, gd)),
                  _mxu_weight(gm_w_out[j]))
            x = _gmlp_mixer(x, pos if i == 0 else None, mod, None, *gm, tm=tm_lat)
            x = _swiglu_ffn(x, mod, None, norm_ffn_w[i], ffn_in, ffn_out,
                            final_norm_w if last else None, tm=tm_ffn)
            if not last:
                ctx_flat = _gmlp_mixer(ctx_flat, None, mod, ctx_row, *gm, tm=tm_ctx)
                ctx_flat = _swiglu_ffn(ctx_flat, mod, ctx_row, norm_ffn_w[i], ffn_in, ffn_out, None, tm=tm_ctx)
        else:
            if i == 0:
                x = x + pos
            w = _mxu_weight(hg_w_in[j])
            scan = functools.partial(_hgrn2_scan, norm_w=norm_mix_w[i], w_in=w, hg_lb=hg_lb, layer=i)
            zero = jnp.zeros((bsz, 2, hk, dk), F32)
            ctx3 = ctx_flat.reshape(bsz, n_ctx, d)
            oc_f, oc_b, s_ctx = scan(ctx3, mod, ctx_row, s0=zero, t=n_ctx)
            o_f, o_b, _ = scan(x, mod, None, s0=s_ctx, t=t_scan)
            readout = functools.partial(_hgrn2_ffn, norm_mix_w=norm_mix_w[i], hg_w_in=w, g_off=4 * hk,
                                        hg_norm_w=hg_norm_w[j], hg_w_out=_mxu_weight(hg_w_out[j]),
                                        norm_ffn_w=norm_ffn_w[i], w_in=ffn_in, w_out=ffn_out)
            x = readout(x, o_f, o_b, mod, None, final_norm_w=final_norm_w if last else None, tm=tm_ffn)
            if not last:
                ctx3 = readout(ctx3, oc_f, oc_b, mod, ctx_row, final_norm_w=None, tm=tm_ctx)
                ctx_flat = ctx3.reshape(1, bsz * n_ctx, d)
    return x
```

```python
import functools

import jax
import jax.numpy as jnp
from jax import lax
from jax.experimental import pallas as pl
from jax.experimental.pallas import tpu as pltpu

NORM_EPS = 1e-6
POS_BASE = 10000.0
GRID_W = 64
N_MIXERS = 2
GM_CHUNK = 128
GM_GROUPS = 8
GROUPS_PER_DOT = 2
HG_HEADS = 8
SCAN_CHUNK = 128
MOD_ROWS = 16

V7X_VMEM_LIMIT_BYTES = 56 * 1024 * 1024
COL_BLOCK = 256
LANES = 128
STRIDED_LOAD_PERIOD = 8

BF16 = jnp.bfloat16
F32 = jnp.float32


def _dot(a, b):
    return jnp.dot(a, b, preferred_element_type=F32)


def _dot_nt(a, b):
    return lax.dot_general(a, b, (((1,), (1,)), ((), ())), preferred_element_type=F32)


def _dot_tn(a, b):
    return lax.dot_general(a, b, (((0,), (0,)), ((), ())), preferred_element_type=F32)


def _ada_norm(x, w, shift, scale):
    ms = jnp.mean(x * x, axis=-1, keepdims=True)
    return (x * lax.rsqrt(ms + NORM_EPS) * w) * (1.0 + scale) + shift


def _gelu_exact(x):
    return 0.5 * x * (1.0 + lax.erf(x * (2.0 ** -0.5)))


def _const_spec(shape):
    return pl.BlockSpec(shape, lambda *_: (0,) * len(shape), pipeline_mode=pl.Buffered(1))


def _ada_mod_kernel(c_ref, w_ref, b_ref, o_ref):
    c = c_ref[...]
    s = c * jax.nn.sigmoid(c)
    o_ref[...] = jnp.dot(s, w_ref[...], preferred_element_type=F32,
                         precision=lax.Precision.HIGHEST) + b_ref[...]


def _ada_mod(cvec, ada_w, ada_b):
    depth, d, n6 = ada_w.shape
    tn = 1536
    return pl.pallas_call(
        _ada_mod_kernel,
        out_shape=jax.ShapeDtypeStruct((depth, MOD_ROWS, n6), F32),
        grid=(depth, n6 // tn),
        in_specs=[
            pl.BlockSpec((MOD_ROWS, d), lambda l, j: (0, 0)),
            pl.BlockSpec((None, d, tn), lambda l, j: (l, 0, j)),
            pl.BlockSpec((None, 1, tn), lambda l, j: (l, 0, j)),
        ],
        out_specs=pl.BlockSpec((None, MOD_ROWS, tn), lambda l, j: (l, 0, j)),
        compiler_params=pltpu.CompilerParams(
            dimension_semantics=("arbitrary", "arbitrary"),
            vmem_limit_bytes=V7X_VMEM_LIMIT_BYTES),
        name="ada_mod",
    )(cvec, ada_w, ada_b.reshape(depth, 1, n6))


def _gmlp_kernel(add_pos, *refs):
    if add_pos:
        x_ref, pos_ref = refs[0], refs[1]
        refs = refs[2:]
    else:
        x_ref, pos_ref = refs[0], None
        refs = refs[1:]
    (mod_ref, nw_ref, win_ref, bin_ref, lng_ref, lnb_ref, ws_ref, bs_ref, wout_ref,
     o_ref, h_scr, z_scr, vn_scr, p_scr) = refs
    tm, d = x_ref.shape
    half = lng_ref.shape[-1]
    gd = half // GM_GROUPS
    n_chunks = tm // GM_CHUNK

    x = x_ref[...]
    if add_pos:
        x = x + pos_ref[...]
    h_scr[...] = _ada_norm(x, nw_ref[...], mod_ref[0:1, :], mod_ref[1:2, :]).astype(BF16)

    def z_block(j):
        cs = slice(j * COL_BLOCK, (j + 1) * COL_BLOCK)
        z = _dot(h_scr[...], win_ref[:, cs]) + bin_ref[:, cs]
        z_scr[:, cs] = _gelu_exact(z)

    n_blk = half // COL_BLOCK
    for j in range(n_blk, 2 * n_blk):
        z_block(j)
    v = z_scr[:, half:2 * half]
    mu = jnp.mean(v, axis=-1, keepdims=True)
    vc = v - mu
    rstd = lax.rsqrt(jnp.mean(vc * vc, axis=-1, keepdims=True) + NORM_EPS)
    for j in range(n_blk):
        z_block(j)

    y = None
    for g in range(GM_GROUPS):
        ucs = slice(g * gd, (g + 1) * gd)
        vcs = slice(half + g * gd, half + (g + 1) * gd)
        vn_scr[:, ucs] = ((z_scr[:, vcs] - mu) * rstd * lng_ref[:, ucs] + lnb_ref[:, ucs]).astype(BF16)
        for c in range(n_chunks):
            rs = slice(c * GM_CHUNK, (c + 1) * GM_CHUNK)
            vs = _dot(ws_ref[g], vn_scr[rs, ucs]) + bs_ref[g]
            p_scr[rs, ucs] = (z_scr[rs, ucs] * vs).astype(BF16)
        if (g + 1) % GROUPS_PER_DOT == 0:
            ks = slice((g + 1 - GROUPS_PER_DOT) * gd, (g + 1) * gd)
            part = _dot(p_scr[:, ks], wout_ref[ks, :d])
            y = part if y is None else y + part
    o_ref[...] = x + mod_ref[2:3, :] * y


def _gmlp_mixer(x, pos, mod, mod_row, norm_w, w_in, b_in, ln_g, ln_b, w_s, b_s, w_out, tm):
    nb, n, d = x.shape
    half = ln_g.shape[-1]
    gd = half // GM_GROUPS
    add_pos = pos is not None
    mod_map = (lambda b, i: (b, 0, 0)) if mod_row is None else (lambda b, i: (mod_row, 0, 0))
    in_specs = [pl.BlockSpec((None, tm, d), lambda b, i: (b, i, 0))]
    args = [x]
    if add_pos:
        in_specs.append(pl.BlockSpec((tm, d), lambda b, i: (i, 0)))
        args.append(pos)
    in_specs += [
        pl.BlockSpec((None, 6, d), mod_map),
        _const_spec((1, d)),
        _const_spec(w_in.shape),
        _const_spec((1, 2 * half)),
        _const_spec((1, half)),
        _const_spec((1, half)),
        _const_spec((GM_GROUPS, GM_CHUNK, GM_CHUNK)),
        _const_spec((GM_GROUPS, GM_CHUNK, gd)),
        _const_spec(w_out.shape),
    ]
    args += [mod, norm_w.reshape(1, d), w_in, b_in.reshape(1, -1), ln_g.reshape(1, -1), ln_b.reshape(1, -1),
             w_s, b_s, w_out]
    return pl.pallas_call(
        functools.partial(_gmlp_kernel, add_pos),
        out_shape=jax.ShapeDtypeStruct((nb, n, d), F32),
        grid=(nb, n // tm),
        in_specs=in_specs,
        out_specs=pl.BlockSpec((None, tm, d), lambda b, i: (b, i, 0)),
        scratch_shapes=[pltpu.VMEM((tm, d), BF16), pltpu.VMEM((tm, 2 * half + LANES), F32),
                        pltpu.VMEM((tm, half + LANES), BF16), pltpu.VMEM((tm, half + LANES), BF16)],
        compiler_params=pltpu.CompilerParams(
            dimension_semantics=("arbitrary", "arbitrary"),
            vmem_limit_bytes=V7X_VMEM_LIMIT_BYTES),
        name="gmlp_mixer",
    )(*args)


def _swiglu_residual(x, mod_ref, nw_ref, win_ref, wout_ref, h_scr, g_scr):
    d = x.shape[-1]
    d_ff = g_scr.shape[-1]
    h_scr[...] = _ada_norm(x, nw_ref[...], mod_ref[3:4, :], mod_ref[4:5, :]).astype(BF16)
    for j in range(d_ff // COL_BLOCK):
        cs = slice(j * COL_BLOCK, (j + 1) * COL_BLOCK)
        a = _dot(h_scr[...], win_ref[:, cs])
        b = _dot(h_scr[...], win_ref[:, d_ff + j * COL_BLOCK:d_ff + (j + 1) * COL_BLOCK])
        g_scr[:, cs] = (a * jax.nn.sigmoid(a) * b).astype(BF16)
    y = _dot(g_scr[...], wout_ref[:, :d])
    return x + mod_ref[5:6, :] * y


def _rms_norm(x, w):
    ms = jnp.mean(x * x, axis=-1, keepdims=True)
    return x * lax.rsqrt(ms + NORM_EPS) * w


def _ffn_kernel(final_norm, x_ref, mod_ref, nw_ref, win_ref, wout_ref, *rest):
    if final_norm:
        fnw_ref, o_ref, h_scr, g_scr = rest
    else:
        o_ref, h_scr, g_scr = rest
    x = _swiglu_residual(x_ref[...], mod_ref, nw_ref, win_ref, wout_ref, h_scr, g_scr)
    if final_norm:
        x = _rms_norm(x, fnw_ref[...])
    o_ref[...] = x


def _swiglu_ffn(x, mod, mod_row, norm_w, w_in, w_out, final_norm_w, tm):
    nb, n, d = x.shape
    d_ff = w_out.shape[0]
    final_norm = final_norm_w is not None
    mod_map = (lambda b, i: (b, 0, 0)) if mod_row is None else (lambda b, i: (mod_row, 0, 0))
    in_specs = [
        pl.BlockSpec((None, tm, d), lambda b, i: (b, i, 0)),
        pl.BlockSpec((None, 6, d), mod_map),
        _const_spec((1, d)),
        _const_spec(w_in.shape),
        _const_spec(w_out.shape),
    ]
    args = [x, mod, norm_w.reshape(1, d), w_in, w_out]
    if final_norm:
        in_specs.append(_const_spec((1, d)))
        args.append(final_norm_w.reshape(1, d))
    return pl.pallas_call(
        functools.partial(_ffn_kernel, final_norm),
        out_shape=jax.ShapeDtypeStruct((nb, n, d), F32),
        grid=(nb, n // tm),
        in_specs=in_specs,
        out_specs=pl.BlockSpec((None, tm, d), lambda b, i: (b, i, 0)),
        scratch_shapes=[pltpu.VMEM((tm, d), BF16), pltpu.VMEM((tm, d_ff), BF16)],
        compiler_params=pltpu.CompilerParams(
            dimension_semantics=("arbitrary", "arbitrary"),
            vmem_limit_bytes=V7X_VMEM_LIMIT_BYTES),
        name="swiglu_ffn",
    )(*args)


def _lower_bounds(lb_ref, layer, depth):
    out = []
    for dirn in range(2):
        logits = [lb_ref[2 * l + dirn:2 * l + dirn + 1, :] for l in range(depth)]
        m = functools.reduce(jnp.maximum, logits)
        e = [jnp.exp(t - m) for t in logits]
        denom = functools.reduce(lambda a, b: a + b, e)
        num = jnp.zeros_like(m)
        for l in range(1, layer + 1):
            num = num + e[l]
        out.append(num / denom)
    return out


LOG2_E = 1.4426950408889634


def _forget(z, lb):
    t = jnp.exp(-jnp.abs(z))
    r = 1.0 / (1.0 + t)
    tr = t * r
    pos = z >= 0.0
    one_m = 1.0 - lb
    f = lb + one_m * jnp.where(pos, r, tr)
    key = one_m * jnp.where(pos, tr, r)
    return key, jnp.log(f) * LOG2_E


def _scan_tile(q_scr, k_scr, lf_scr, v_scr, hm_scr, lq_scr, qa_scr, ki_scr, kd_scr, dec_scr, u_scr, rv_scr,
               st_ref, o_ref, rev):
    t = q_scr.shape[0]
    hk, dk = st_ref.shape
    pitch = dk + SCAN_CHUNK
    heads_per_block = COL_BLOCK // dk
    n_chunks = t // SCAN_CHUNK
    row = lax.broadcasted_iota(jnp.int32, (SCAN_CHUNK, SCAN_CHUNK), 0)
    col = lax.broadcasted_iota(jnp.int32, (SCAN_CHUNK, SCAN_CHUNK), 1)
    tri = (row <= col) if rev else (row >= col)
    tri_bf = tri.astype(BF16)
    last = 0 if rev else SCAN_CHUNK - 1
    mid = SCAN_CHUNK // 2 if rev else SCAN_CHUNK // 2 - 1
    order = range(n_chunks - 1, -1, -1) if rev else range(n_chunks)
    rows = [slice(c * SCAN_CHUNK, (c + 1) * SCAN_CHUNK) for c in range(n_chunks)]
    cols = [slice(hh * dk, (hh + 1) * dk) for hh in range(HG_HEADS)]

    def decays():
        for c in order:
            rs = rows[c]
            for j in range(hk // COL_BLOCK):
                cb = slice(j * COL_BLOCK, (j + 1) * COL_BLOCK)
                cb2 = slice(hk + j * COL_BLOCK, hk + (j + 1) * COL_BLOCK)
                lf = lf_scr[rs, cb]
                hi = lf.astype(BF16)
                hm_scr[rs, cb] = hi
                hm_scr[rs, cb2] = (lf - hi.astype(F32)).astype(BF16)
                b = _dot(tri_bf, hm_scr[rs, cb]) + _dot(tri_bf, hm_scr[rs, cb2])
                b_last = b[last:last + 1, :]
                b_mid = b[mid:mid + 1, :]
                kf = k_scr[rs, cb]
                qf = q_scr[rs, cb]
                qd = (qf * jnp.exp2(b)).astype(BF16)
                for k in range(heads_per_block):
                    hh = j * heads_per_block + k
                    lq_scr[rs, hh * pitch:hh * pitch + dk] = qd[:, k * dk:(k + 1) * dk]
                qa_scr[rs, cb] = (qf * jnp.exp2(b - b_mid)).astype(BF16)
                ki_scr[rs, cb] = (kf * jnp.exp2(b_mid - b)).astype(BF16)
                kd_scr[rs, cb] = (kf * jnp.exp2(b_last - b)).astype(BF16)
                dec_scr[c:c + 1, cb] = jnp.exp2(b_last)

    def scores():
        for c in order:
            rs = rows[c]
            for hh, cs in enumerate(cols):
                att = _dot_nt(qa_scr[rs, cs], ki_scr[rs, cs])
                lq_scr[rs, hh * pitch + dk:hh * pitch + dk + SCAN_CHUNK] = jnp.where(tri, att, 0.0).astype(BF16)
                rv_scr[c, hh, dk:, :] = v_scr[rs, cs]
                u_scr[c, cs, :] = _dot_tn(v_scr[rs, cs], kd_scr[rs, cs])

    def recurrence():
        for c in order:
            for hh, cs in enumerate(cols):
                st = st_ref[cs, :]
                rv_scr[c, hh, :dk, :] = st.T.astype(BF16)
                st_ref[cs, :] = st * dec_scr[c:c + 1, cs] + u_scr[c, cs, :]

    def outputs():
        for c in order:
            rs = rows[c]
            for hh, cs in enumerate(cols):
                o = _dot(lq_scr[rs, hh * pitch:hh * pitch + dk + SCAN_CHUNK], rv_scr[c, hh])
                o_ref[rs, cs] = o.astype(o_ref.dtype)

    return decays, scores, recurrence, outputs


def _hgrn2_scan_kernel(layer, depth, pipelined, xf_ref, xb_ref, mod_ref, nw_ref, w_ref,
                       lb_ref, s0_ref, of_ref, ob_ref, sfin_ref,
                       st_scr, h_scr, q_scr, k_scr, lf_scr, v_scr, hm_scr, lq_scr, qa_scr, ki_scr, kd_scr,
                       dec_scr, u_scr, rv_scr):
    i = pl.program_id(1)

    @pl.when(i == 0)
    def _():
        st_scr[...] = s0_ref[...]
        if pipelined:
            q_scr[1] = jnp.zeros(q_scr.shape[1:], q_scr.dtype)
            k_scr[1] = jnp.zeros(k_scr.shape[1:], k_scr.dtype)
            lf_scr[1] = jnp.zeros(lf_scr.shape[1:], lf_scr.dtype)
            v_scr[1] = jnp.zeros(v_scr.shape[1:], v_scr.dtype)

    hk = lb_ref.shape[-1]

    def step(slot_in, slot_out):
        lbs = _lower_bounds(lb_ref, layer, depth)

        def project(dirn, x_ref, j):
            h_d = h_scr.at[dirn]
            if j == 0:
                h_d[...] = _ada_norm(x_ref[...], nw_ref[...], mod_ref[0:1, :], mod_ref[1:2, :]).astype(BF16)
            q_in, k_in, lf_in, v_in = (r.at[slot_in, dirn] for r in (q_scr, k_scr, lf_scr, v_scr))
            cb = slice(j * COL_BLOCK, (j + 1) * COL_BLOCK)
            q = _dot(h_d[...], w_ref[:, cb])
            q_in[:, cb] = q * jax.nn.sigmoid(q)
            f_off = (1 + dirn) * hk + j * COL_BLOCK
            key, log2_f = _forget(_dot(h_d[...], w_ref[:, f_off:f_off + COL_BLOCK]), lbs[dirn][:, cb])
            k_in[:, cb] = key
            lf_in[:, cb] = log2_f
            i_off = 3 * hk + j * COL_BLOCK
            v_in[:, cb] = _dot(h_d[...], w_ref[:, i_off:i_off + COL_BLOCK]).astype(BF16)

        stages = []
        projections = []
        for dirn, (x_ref, o_ref) in enumerate(((xf_ref, of_ref), (xb_ref, ob_ref))):
            projections += [functools.partial(project, dirn, x_ref, j) for j in range(hk // COL_BLOCK)]
            stages.append(_scan_tile(
                q_scr.at[slot_out, dirn], k_scr.at[slot_out, dirn], lf_scr.at[slot_out, dirn],
                v_scr.at[slot_out, dirn], hm_scr.at[dirn], lq_scr.at[dirn], qa_scr.at[dirn], ki_scr.at[dirn],
                kd_scr.at[dirn],
                dec_scr.at[dirn], u_scr.at[dirn], rv_scr.at[dirn], st_scr.at[dirn], o_ref, rev=(dirn == 1)))
        for projection in projections:
            projection()
        for stage_pair in zip(*stages):
            for stage in stage_pair:
                stage()

    if pipelined:
        for parity in (0, 1):
            pl.when(lax.rem(i, 2) == parity)(functools.partial(step, parity, 1 - parity))
    else:
        step(0, 0)

    @pl.when(i == pl.num_programs(1) - 1)
    def _():
        sfin_ref[...] = st_scr[...]


def _hgrn2_scan(x, mod, mod_row, norm_w, w_in, hg_lb, layer, s0, t):
    nb, n, d = x.shape
    hk = hg_lb.shape[-1]
    dk = hk // HG_HEADS
    nt = n // t
    nc = t // SCAN_CHUNK
    depth = hg_lb.shape[0]
    mod_map = (lambda b, i: (b, 0, 0)) if mod_row is None else (lambda b, i: (mod_row, 0, 0))
    lb2 = hg_lb.reshape(depth * 2, hk)

    pipelined = nt > 1
    n_sets = 2 if pipelined else 1

    def tile_in(i):
        return jnp.minimum(i, nt - 1) if pipelined else i

    def tile_out(i):
        return jnp.maximum(i - 1, 0) if pipelined else i

    return pl.pallas_call(
        functools.partial(_hgrn2_scan_kernel, layer, depth, pipelined),
        out_shape=(jax.ShapeDtypeStruct((nb, n, hk), BF16),
                   jax.ShapeDtypeStruct((nb, n, hk), BF16),
                   jax.ShapeDtypeStruct((nb, 2, hk, dk), F32)),
        grid=(nb, nt + 1 if pipelined else nt),
        in_specs=[
            pl.BlockSpec((None, t, d), lambda b, i: (b, tile_in(i), 0)),
            pl.BlockSpec((None, t, d), lambda b, i: (b, nt - 1 - tile_in(i), 0)),
            pl.BlockSpec((None, 6, d), mod_map),
            _const_spec((1, d)),
            _const_spec(w_in.shape),
            _const_spec((depth * 2, hk)),
            pl.BlockSpec((None, 2, hk, dk), lambda b, i: (b, 0, 0, 0)),
        ],
        out_specs=(
            pl.BlockSpec((None, t, hk), lambda b, i: (b, tile_out(i), 0)),
            pl.BlockSpec((None, t, hk), lambda b, i: (b, nt - 1 - tile_out(i), 0)),
            pl.BlockSpec((None, 2, hk, dk), lambda b, i: (b, 0, 0, 0)),
        ),
        scratch_shapes=[
            pltpu.VMEM((2, hk, dk), F32),
            pltpu.VMEM((2, t, d), BF16),
            pltpu.VMEM((n_sets, 2, t, hk + LANES), F32),
            pltpu.VMEM((n_sets, 2, t, hk + LANES), F32),
            pltpu.VMEM((n_sets, 2, t, hk + LANES), F32),
            pltpu.VMEM((n_sets, 2, t, hk + LANES), BF16),
            pltpu.VMEM((2, t, 2 * hk + LANES), BF16),
            pltpu.VMEM((2, t, HG_HEADS * (dk + SCAN_CHUNK) + LANES), BF16),
            pltpu.VMEM((2, t, hk + LANES), BF16),
            pltpu.VMEM((2, t, hk + LANES), BF16),
            pltpu.VMEM((2, t, hk + LANES), BF16),
            pltpu.VMEM((2, max(nc, 8), hk), F32),
            pltpu.VMEM((2, nc, hk, dk), F32),
            pltpu.VMEM((2, nc, HG_HEADS, dk + SCAN_CHUNK, dk), BF16),
        ],
        compiler_params=pltpu.CompilerParams(
            dimension_semantics=("arbitrary", "arbitrary"),
            vmem_limit_bytes=V7X_VMEM_LIMIT_BYTES),
        name="hgrn2_scan",
    )(x, x, mod, norm_w.reshape(1, d), w_in, lb2, s0)


def _hgrn2_ffn_kernel(final_norm, g_off, x_ref, of_ref, ob_ref, mod_ref, nwm_ref, wg_ref, hnw_ref, wo_ref,
                      nwf_ref, win_ref, wout_ref, *rest):
    if final_norm:
        fnw_ref, o_ref, h_scr, og_scr, g_scr = rest
    else:
        o_ref, h_scr, og_scr, g_scr = rest
    x = x_ref[...]
    hv = of_ref.shape[-1]
    dv = hv // HG_HEADS
    d = x.shape[-1]
    h_scr[...] = _ada_norm(x, nwm_ref[...], mod_ref[0:1, :], mod_ref[1:2, :]).astype(BF16)
    for j in range(hv // COL_BLOCK):
        gate = _dot(h_scr[...], wg_ref[:, g_off + j * COL_BLOCK:g_off + (j + 1) * COL_BLOCK])
        gate = gate * jax.nn.sigmoid(gate)
        for hh in range(j * COL_BLOCK // dv, (j + 1) * COL_BLOCK // dv):
            cs = slice(hh * dv, (hh + 1) * dv)
            oh = of_ref[:, cs].astype(F32) + ob_ref[:, cs].astype(F32)
            on = _rms_norm(oh, hnw_ref[:, cs])
            og_scr[:, cs] = (on * gate[:, hh * dv - j * COL_BLOCK:(hh + 1) * dv - j * COL_BLOCK]).astype(BF16)
    y = _dot(og_scr[...], wo_ref[:, :d])
    x = x + mod_ref[2:3, :] * y
    x = _swiglu_residual(x, mod_ref, nwf_ref, win_ref, wout_ref, h_scr, g_scr)
    if final_norm:
        x = _rms_norm(x, fnw_ref[...])
    o_ref[...] = x


def _hgrn2_ffn(x, o_f, o_b, mod, mod_row, norm_mix_w, hg_w_in, g_off, hg_norm_w, hg_w_out, norm_ffn_w,
               w_in, w_out, final_norm_w, tm):
    nb, n, d = x.shape
    hv = o_f.shape[-1]
    d_ff = w_out.shape[0]
    final_norm = final_norm_w is not None
    mod_map = (lambda b, i: (b, 0, 0)) if mod_row is None else (lambda b, i: (mod_row, 0, 0))
    in_specs = [
        pl.BlockSpec((None, tm, d), lambda b, i: (b, i, 0)),
        pl.BlockSpec((None, tm, hv), lambda b, i: (b, i, 0)),
        pl.BlockSpec((None, tm, hv), lambda b, i: (b, i, 0)),
        pl.BlockSpec((None, 6, d), mod_map),
        _const_spec((1, d)),
        _const_spec(hg_w_in.shape),
        _const_spec((1, hv)),
        _const_spec(hg_w_out.shape),
        _const_spec((1, d)),
        _const_spec(w_in.shape),
        _const_spec(w_out.shape),
    ]
    args = [x, o_f, o_b, mod, norm_mix_w.reshape(1, d), hg_w_in, hg_norm_w.reshape(1, hv), hg_w_out,
            norm_ffn_w.reshape(1, d), w_in, w_out]
    if final_norm:
        in_specs.append(_const_spec((1, d)))
        args.append(final_norm_w.reshape(1, d))
    return pl.pallas_call(
        functools.partial(_hgrn2_ffn_kernel, final_norm, g_off),
        out_shape=jax.ShapeDtypeStruct((nb, n, d), F32),
        grid=(nb, n // tm),
        in_specs=in_specs,
        out_specs=pl.BlockSpec((None, tm, d), lambda b, i: (b, i, 0)),
        scratch_shapes=[pltpu.VMEM((tm, d), BF16), pltpu.VMEM((tm, hv), BF16), pltpu.VMEM((tm, d_ff), BF16)],
        compiler_params=pltpu.CompilerParams(
            dimension_semantics=("arbitrary", "arbitrary"),
            vmem_limit_bytes=V7X_VMEM_LIMIT_BYTES),
        name="hgrn2_ffn",
    )(*args)


def _sincos(pos, dim):
    half = dim // 2
    omega = 1.0 / (POS_BASE ** (jnp.arange(half, dtype=F32) / half))
    ang = pos.astype(F32)[:, None] * omega[None, :]
    return jnp.concatenate([jnp.sin(ang), jnp.cos(ang)], axis=-1)


def _grid_pos_code(n, d):
    rows = n // GRID_W
    half = d // 2
    row_code = _sincos(jnp.arange(rows), half)
    col_code = _sincos(jnp.arange(GRID_W), half)
    code = jnp.concatenate([
        jnp.broadcast_to(row_code[:, None, :], (rows, GRID_W, half)),
        jnp.broadcast_to(col_code[None, :, :], (rows, GRID_W, half))], axis=-1)
    return code.reshape(rows * GRID_W, d)


def _mxu_weight(w):
    w = w.astype(BF16)
    if (w.shape[-1] // LANES) % STRIDED_LOAD_PERIOD == 0:
        w = jnp.pad(w, ((0, 0), (0, LANES)))
    return w


def _pick_tile(n, target):
    t = min(n, target)
    while n % t:
        t //= 2
    return t


def kernel(x, c, ctx, c_ctx, ada_w, ada_b, norm_mix_w, norm_ffn_w, gm_w_in, gm_b_in, gm_ln_g, gm_ln_b,
           gm_w_s, gm_b_s, gm_w_out, hg_w_in, hg_lb, hg_norm_w, hg_w_out, ffn_w_in, ffn_w_out, final_norm_w):
    bsz, n, d = x.shape
    n_ctx = ctx.shape[1]
    depth = ada_w.shape[0]
    assert bsz + 1 <= MOD_ROWS
    hk = hg_lb.shape[-1]
    dk = hk // HG_HEADS

    cvec = jnp.zeros((MOD_ROWS, d), F32).at[:bsz].set(c).at[bsz].set(c_ctx)
    mod_all = _ada_mod(cvec, ada_w, ada_b).reshape(depth, MOD_ROWS, 6, d)
    ctx_row = bsz

    pos = _grid_pos_code(n, d)
    ctx_flat = ctx.reshape(1, bsz * n_ctx, d)

    tm_lat = _pick_tile(n, 256)
    tm_ffn = _pick_tile(n, 512)
    tm_ctx = _pick_tile(n_ctx, 256)
    t_scan = _pick_tile(n, 256)

    for i in range(depth):
        last = i == depth - 1
        use_a = i % N_MIXERS == 0
        j = i // N_MIXERS
        mod = mod_all[i]
        ffn_in = _mxu_weight(ffn_w_in[i])
        ffn_out = _mxu_weight(ffn_w_out[i])

        if use_a:
            half = gm_ln_g.shape[-1]
            gd = half // GM_GROUPS
            gm = (norm_mix_w[i], _mxu_weight(gm_w_in[j]), gm_b_in[j], gm_ln_g[j], gm_ln_b[j],
                  gm_w_s[j].astype(BF16),
                  jnp.broadcast_to(gm_b_s[j][:, :, None], (GM_GROUPS, GM_CHUNK, gd)),
                  _mxu_weight(gm_w_out[j]))
            x = _gmlp_mixer(x, pos if i == 0 else None, mod, None, *gm, tm=tm_lat)
            x = _swiglu_ffn(x, mod, None, norm_ffn_w[i], ffn_in, ffn_out,
                            final_norm_w if last else None, tm=tm_ffn)
            if not last:
                ctx_flat = _gmlp_mixer(ctx_flat, None, mod, ctx_row, *gm, tm=tm_ctx)
                ctx_flat = _swiglu_ffn(ctx_flat, mod, ctx_row, norm_ffn_w[i], ffn_in, ffn_out, None, tm=tm_ctx)
        else:
            if i == 0:
                x = x + pos
            w = _mxu_weight(hg_w_in[j])
            scan = functools.partial(_hgrn2_scan, norm_w=norm_mix_w[i], w_in=w, hg_lb=hg_lb, layer=i)
            zero = jnp.zeros((bsz, 2, hk, dk), F32)
            ctx3 = ctx_flat.reshape(bsz, n_ctx, d)
            oc_f, oc_b, s_ctx = scan(ctx3, mod, ctx_row, s0=zero, t=n_ctx)
            o_f, o_b, _ = scan(x, mod, None, s0=s_ctx, t=t_scan)
            readout = functools.partial(_hgrn2_ffn, norm_mix_w=norm_mix_w[i], hg_w_in=w, g_off=4 * hk,
                                        hg_norm_w=hg_norm_w[j], hg_w_out=_mxu_weight(hg_w_out[j]),
                                        norm_ffn_w=norm_ffn_w[i], w_in=ffn_in, w_out=ffn_out)
            x = readout(x, o_f, o_b, mod, None, final_norm_w=final_norm_w if last else None, tm=tm_ffn)
            if not last:
                ctx3 = readout(ctx3, oc_f, oc_b, mod, ctx_row, final_norm_w=None, tm=tm_ctx)
                ctx_flat = ctx3.reshape(1, bsz * n_ctx, d)
    return x
```

```python
import functools

import jax
import jax.numpy as jnp
from jax import lax
from jax.experimental import pallas as pl
from jax.experimental.pallas import tpu as pltpu

NORM_EPS = 1e-6
POS_BASE = 10000.0
GRID_W = 64
N_MIXERS = 2
GM_CHUNK = 128
GM_GROUPS = 8
GROUPS_PER_DOT = 2
HG_HEADS = 8
SCAN_CHUNK = 128
MOD_ROWS = 16

V7X_VMEM_LIMIT_BYTES = 56 * 1024 * 1024
COL_BLOCK = 256
LANES = 128
STRIDED_LOAD_PERIOD = 8

BF16 = jnp.bfloat16
F32 = jnp.float32


def _dot(a, b):
    return jnp.dot(a, b, preferred_element_type=F32)


def _dot_nt(a, b):
    return lax.dot_general(a, b, (((1,), (1,)), ((), ())), preferred_element_type=F32)


def _dot_tn(a, b):
    return lax.dot_general(a, b, (((0,), (0,)), ((), ())), preferred_element_type=F32)


def _ada_norm(x, w, shift, scale):
    ms = jnp.mean(x * x, axis=-1, keepdims=True)
    return (x * lax.rsqrt(ms + NORM_EPS) * w) * (1.0 + scale) + shift


def _silu(x):
    return x * (0.5 + 0.5 * jnp.tanh(0.5 * x))


def _gelu_exact(x):
    return 0.5 * x * (1.0 + lax.erf(x * (2.0 ** -0.5)))


def _const_spec(shape):
    return pl.BlockSpec(shape, lambda *_: (0,) * len(shape), pipeline_mode=pl.Buffered(1))


def _ada_mod_kernel(c_ref, w_ref, b_ref, o_ref):
    c = c_ref[...]
    s = c * jax.nn.sigmoid(c)
    o_ref[...] = jnp.dot(s, w_ref[...], preferred_element_type=F32,
                         precision=lax.Precision.HIGHEST) + b_ref[...]


def _ada_mod(cvec, ada_w, ada_b):
    depth, d, n6 = ada_w.shape
    tn = 1536
    return pl.pallas_call(
        _ada_mod_kernel,
        out_shape=jax.ShapeDtypeStruct((depth, MOD_ROWS, n6), F32),
        grid=(depth, n6 // tn),
        in_specs=[
            pl.BlockSpec((MOD_ROWS, d), lambda l, j: (0, 0)),
            pl.BlockSpec((None, d, tn), lambda l, j: (l, 0, j)),
            pl.BlockSpec((None, 1, tn), lambda l, j: (l, 0, j)),
        ],
        out_specs=pl.BlockSpec((None, MOD_ROWS, tn), lambda l, j: (l, 0, j)),
        compiler_params=pltpu.CompilerParams(
            dimension_semantics=("arbitrary", "arbitrary"),
            vmem_limit_bytes=V7X_VMEM_LIMIT_BYTES),
        name="ada_mod",
    )(cvec, ada_w, ada_b.reshape(depth, 1, n6))


def _gmlp_kernel(add_pos, *refs):
    if add_pos:
        x_ref, pos_ref = refs[0], refs[1]
        refs = refs[2:]
    else:
        x_ref, pos_ref = refs[0], None
        refs = refs[1:]
    (mod_ref, nw_ref, win_ref, bin_ref, lng_ref, lnb_ref, ws_ref, bs_ref, wout_ref,
     o_ref, h_scr, z_scr, vn_scr, p_scr) = refs
    tm, d = x_ref.shape
    half = lng_ref.shape[-1]
    gd = half // GM_GROUPS
    n_chunks = tm // GM_CHUNK

    x = x_ref[...]
    if add_pos:
        x = x + pos_ref[...]
    h_scr[...] = _ada_norm(x, nw_ref[...], mod_ref[0:1, :], mod_ref[1:2, :]).astype(BF16)

    def z_block(j):
        cs = slice(j * COL_BLOCK, (j + 1) * COL_BLOCK)
        z = _dot(h_scr[...], win_ref[:, cs]) + bin_ref[:, cs]
        z_scr[:, cs] = _gelu_exact(z)

    n_blk = half // COL_BLOCK
    for j in range(n_blk, 2 * n_blk):
        z_block(j)
    v = z_scr[:, half:2 * half]
    mu = jnp.mean(v, axis=-1, keepdims=True)
    vc = v - mu
    rstd = lax.rsqrt(jnp.mean(vc * vc, axis=-1, keepdims=True) + NORM_EPS)
    for j in range(n_blk):
        z_block(j)

    y = None
    for g in range(GM_GROUPS):
        ucs = slice(g * gd, (g + 1) * gd)
        vcs = slice(half + g * gd, half + (g + 1) * gd)
        vn_scr[:, ucs] = ((z_scr[:, vcs] - mu) * rstd * lng_ref[:, ucs] + lnb_ref[:, ucs]).astype(BF16)
        for c in range(n_chunks):
            rs = slice(c * GM_CHUNK, (c + 1) * GM_CHUNK)
            vs = _dot(ws_ref[g], vn_scr[rs, ucs]) + bs_ref[g]
            p_scr[rs, ucs] = (z_scr[rs, ucs] * vs).astype(BF16)
        if (g + 1) % GROUPS_PER_DOT == 0:
            ks = slice((g + 1 - GROUPS_PER_DOT) * gd, (g + 1) * gd)
            part = _dot(p_scr[:, ks], wout_ref[ks, :d])
            y = part if y is None else y + part
    o_ref[...] = x + mod_ref[2:3, :] * y


def _gmlp_mixer(x, pos, mod, mod_row, norm_w, w_in, b_in, ln_g, ln_b, w_s, b_s, w_out, tm):
    nb, n, d = x.shape
    half = ln_g.shape[-1]
    gd = half // GM_GROUPS
    add_pos = pos is not None
    mod_map = (lambda b, i: (b, 0, 0)) if mod_row is None else (lambda b, i: (mod_row, 0, 0))
    in_specs = [pl.BlockSpec((None, tm, d), lambda b, i: (b, i, 0))]
    args = [x]
    if add_pos:
        in_specs.append(pl.BlockSpec((tm, d), lambda b, i: (i, 0)))
        args.append(pos)
    in_specs += [
        pl.BlockSpec((None, 6, d), mod_map),
        _const_spec((1, d)),
        _const_spec(w_in.shape),
        _const_spec((1, 2 * half)),
        _const_spec((1, half)),
        _const_spec((1, half)),
        _const_spec((GM_GROUPS, GM_CHUNK, GM_CHUNK)),
        _const_spec((GM_GROUPS, GM_CHUNK, gd)),
        _const_spec(w_out.shape),
    ]
    args += [mod, norm_w.reshape(1, d), w_in, b_in.reshape(1, -1), ln_g.reshape(1, -1), ln_b.reshape(1, -1),
             w_s, b_s, w_out]
    return pl.pallas_call(
        functools.partial(_gmlp_kernel, add_pos),
        out_shape=jax.ShapeDtypeStruct((nb, n, d), F32),
        grid=(nb, n // tm),
        in_specs=in_specs,
        out_specs=pl.BlockSpec((None, tm, d), lambda b, i: (b, i, 0)),
        scratch_shapes=[pltpu.VMEM((tm, d), BF16), pltpu.VMEM((tm, 2 * half + LANES), F32),
                        pltpu.VMEM((tm, half + LANES), BF16), pltpu.VMEM((tm, half + LANES), BF16)],
        compiler_params=pltpu.CompilerParams(
            dimension_semantics=("arbitrary", "arbitrary"),
            vmem_limit_bytes=V7X_VMEM_LIMIT_BYTES),
        name="gmlp_mixer",
    )(*args)


def _swiglu_residual(x, mod_ref, nw_ref, win_ref, wout_ref, h_scr, g_scr):
    d = x.shape[-1]
    d_ff = g_scr.shape[-1]
    h_scr[...] = _ada_norm(x, nw_ref[...], mod_ref[3:4, :], mod_ref[4:5, :]).astype(BF16)
    for j in range(d_ff // COL_BLOCK):
        cs = slice(j * COL_BLOCK, (j + 1) * COL_BLOCK)
        a = _dot(h_scr[...], win_ref[:, cs])
        b = _dot(h_scr[...], win_ref[:, d_ff + j * COL_BLOCK:d_ff + (j + 1) * COL_BLOCK])
        g_scr[:, cs] = (_silu(a) * b).astype(BF16)
    y = _dot(g_scr[...], wout_ref[:, :d])
    return x + mod_ref[5:6, :] * y


def _rms_norm(x, w):
    ms = jnp.mean(x * x, axis=-1, keepdims=True)
    return x * lax.rsqrt(ms + NORM_EPS) * w


def _ffn_kernel(final_norm, x_ref, mod_ref, nw_ref, win_ref, wout_ref, *rest):
    if final_norm:
        fnw_ref, o_ref, h_scr, g_scr = rest
    else:
        o_ref, h_scr, g_scr = rest
    x = _swiglu_residual(x_ref[...], mod_ref, nw_ref, win_ref, wout_ref, h_scr, g_scr)
    if final_norm:
        x = _rms_norm(x, fnw_ref[...])
    o_ref[...] = x


def _swiglu_ffn(x, mod, mod_row, norm_w, w_in, w_out, final_norm_w, tm):
    nb, n, d = x.shape
    d_ff = w_out.shape[0]
    final_norm = final_norm_w is not None
    mod_map = (lambda b, i: (b, 0, 0)) if mod_row is None else (lambda b, i: (mod_row, 0, 0))
    in_specs = [
        pl.BlockSpec((None, tm, d), lambda b, i: (b, i, 0)),
        pl.BlockSpec((None, 6, d), mod_map),
        _const_spec((1, d)),
        _const_spec(w_in.shape),
        _const_spec(w_out.shape),
    ]
    args = [x, mod, norm_w.reshape(1, d), w_in, w_out]
    if final_norm:
        in_specs.append(_const_spec((1, d)))
        args.append(final_norm_w.reshape(1, d))
    return pl.pallas_call(
        functools.partial(_ffn_kernel, final_norm),
        out_shape=jax.ShapeDtypeStruct((nb, n, d), F32),
        grid=(nb, n // tm),
        in_specs=in_specs,
        out_specs=pl.BlockSpec((None, tm, d), lambda b, i: (b, i, 0)),
        scratch_shapes=[pltpu.VMEM((tm, d), BF16), pltpu.VMEM((tm, d_ff), BF16)],
        compiler_params=pltpu.CompilerParams(
            dimension_semantics=("arbitrary", "arbitrary"),
            vmem_limit_bytes=V7X_VMEM_LIMIT_BYTES),
        name="swiglu_ffn",
    )(*args)


def _lower_bounds(lb_ref, layer, depth):
    out = []
    for dirn in range(2):
        logits = [lb_ref[2 * l + dirn:2 * l + dirn + 1, :] for l in range(depth)]
        m = functools.reduce(jnp.maximum, logits)
        e = [jnp.exp(t - m) for t in logits]
        denom = functools.reduce(lambda a, b: a + b, e)
        num = jnp.zeros_like(m)
        for l in range(1, layer + 1):
            num = num + e[l]
        out.append(num / denom)
    return out


LOG2_E = 1.4426950408889634


def _forget(z, lb):
    t = jnp.exp(-jnp.abs(z))
    r = 1.0 / (1.0 + t)
    tr = t * r
    pos = z >= 0.0
    one_m = 1.0 - lb
    f = lb + one_m * jnp.where(pos, r, tr)
    key = one_m * jnp.where(pos, tr, r)
    return key, jnp.log(f) * LOG2_E


def _scan_tile(q_scr, k_scr, lf_scr, v_scr, hm_scr, lq_scr, qa_scr, ki_scr, kd_scr, dec_scr, u_scr, rv_scr,
               st_ref, o_ref, rev):
    t = q_scr.shape[0]
    hk, dk = st_ref.shape
    pitch = dk + SCAN_CHUNK
    heads_per_block = COL_BLOCK // dk
    n_chunks = t // SCAN_CHUNK
    row = lax.broadcasted_iota(jnp.int32, (SCAN_CHUNK, SCAN_CHUNK), 0)
    col = lax.broadcasted_iota(jnp.int32, (SCAN_CHUNK, SCAN_CHUNK), 1)
    tri = (row <= col) if rev else (row >= col)
    tri_bf = tri.astype(BF16)
    last = 0 if rev else SCAN_CHUNK - 1
    mid = SCAN_CHUNK // 2 if rev else SCAN_CHUNK // 2 - 1
    order = range(n_chunks - 1, -1, -1) if rev else range(n_chunks)
    rows = [slice(c * SCAN_CHUNK, (c + 1) * SCAN_CHUNK) for c in range(n_chunks)]
    cols = [slice(hh * dk, (hh + 1) * dk) for hh in range(HG_HEADS)]

    def decays():
        for c in order:
            rs = rows[c]
            for j in range(hk // COL_BLOCK):
                cb = slice(j * COL_BLOCK, (j + 1) * COL_BLOCK)
                cb2 = slice(hk + j * COL_BLOCK, hk + (j + 1) * COL_BLOCK)
                lf = lf_scr[rs, cb]
                hi = lf.astype(BF16)
                hm_scr[rs, cb] = hi
                hm_scr[rs, cb2] = (lf - hi.astype(F32)).astype(BF16)
                b = _dot(tri_bf, hm_scr[rs, cb]) + _dot(tri_bf, hm_scr[rs, cb2])
                b_last = b[last:last + 1, :]
                b_mid = b[mid:mid + 1, :]
                kf = k_scr[rs, cb]
                qf = q_scr[rs, cb]
                qa = qf * jnp.exp2(b - b_mid)
                ki = kf * jnp.exp2(b_mid - b)
                qd = (qa * jnp.exp2(b_mid)).astype(BF16)
                for k in range(heads_per_block):
                    hh = j * heads_per_block + k
                    lq_scr[rs, hh * pitch:hh * pitch + dk] = qd[:, k * dk:(k + 1) * dk]
                qa_scr[rs, cb] = qa.astype(BF16)
                ki_scr[rs, cb] = ki.astype(BF16)
                kd_scr[rs, cb] = (ki * jnp.exp2(b_last - b_mid)).astype(BF16)
                dec_scr[c:c + 1, cb] = jnp.exp2(b_last)

    def scores():
        for c in order:
            rs = rows[c]
            for hh, cs in enumerate(cols):
                att = _dot_nt(qa_scr[rs, cs], ki_scr[rs, cs])
                lq_scr[rs, hh * pitch + dk:hh * pitch + dk + SCAN_CHUNK] = jnp.where(tri, att, 0.0).astype(BF16)
                rv_scr[c, hh, dk:, :] = v_scr[rs, cs]
                u_scr[c, cs, :] = _dot_tn(v_scr[rs, cs], kd_scr[rs, cs])

    def recurrence():
        for c in order:
            for hh, cs in enumerate(cols):
                st = st_ref[cs, :]
                rv_scr[c, hh, :dk, :] = st.T.astype(BF16)
                st_ref[cs, :] = st * dec_scr[c:c + 1, cs] + u_scr[c, cs, :]

    def outputs():
        for c in order:
            rs = rows[c]
            for hh, cs in enumerate(cols):
                o = _dot(lq_scr[rs, hh * pitch:hh * pitch + dk + SCAN_CHUNK], rv_scr[c, hh])
                o_ref[rs, cs] = o.astype(o_ref.dtype)

    return decays, scores, recurrence, outputs


def _hgrn2_scan_kernel(layer, depth, xf_ref, xb_ref, mod_ref, nw_ref, w_ref,
                       lb_ref, s0_ref, of_ref, ob_ref, sfin_ref,
                       st_scr, h_scr, q_scr, k_scr, lf_scr, v_scr, hm_scr, lq_scr, qa_scr, ki_scr, kd_scr,
                       dec_scr, u_scr, rv_scr):
    i = pl.program_id(1)

    @pl.when(i == 0)
    def _():
        st_scr[...] = s0_ref[...]

    hk = lb_ref.shape[-1]
    lbs = _lower_bounds(lb_ref, layer, depth)

    def project(dirn, x_ref, j):
        h_d = h_scr.at[dirn]
        if j == 0:
            h_d[...] = _ada_norm(x_ref[...], nw_ref[...], mod_ref[0:1, :], mod_ref[1:2, :]).astype(BF16)
        cb = slice(j * COL_BLOCK, (j + 1) * COL_BLOCK)
        q_scr[dirn, :, cb] = _silu(_dot(h_d[...], w_ref[:, cb]))
        f_off = (1 + dirn) * hk + j * COL_BLOCK
        key, log2_f = _forget(_dot(h_d[...], w_ref[:, f_off:f_off + COL_BLOCK]), lbs[dirn][:, cb])
        k_scr[dirn, :, cb] = key
        lf_scr[dirn, :, cb] = log2_f
        i_off = 3 * hk + j * COL_BLOCK
        v_scr[dirn, :, cb] = _dot(h_d[...], w_ref[:, i_off:i_off + COL_BLOCK]).astype(BF16)

    stages = []
    for dirn, (x_ref, o_ref) in enumerate(((xf_ref, of_ref), (xb_ref, ob_ref))):
        for j in range(hk // COL_BLOCK):
            project(dirn, x_ref, j)
        stages.append(_scan_tile(
            q_scr.at[dirn], k_scr.at[dirn], lf_scr.at[dirn], v_scr.at[dirn], hm_scr.at[dirn], lq_scr.at[dirn],
            qa_scr.at[dirn], ki_scr.at[dirn], kd_scr.at[dirn], dec_scr.at[dirn], u_scr.at[dirn], rv_scr.at[dirn],
            st_scr.at[dirn], o_ref, rev=(dirn == 1)))
    for stage_pair in zip(*stages):
        for stage in stage_pair:
            stage()

    @pl.when(i == pl.num_programs(1) - 1)
    def _():
        sfin_ref[...] = st_scr[...]


def _hgrn2_scan(x, mod, mod_row, norm_w, w_in, hg_lb, layer, s0, t):
    nb, n, d = x.shape
    hk = hg_lb.shape[-1]
    dk = hk // HG_HEADS
    nt = n // t
    nc = t // SCAN_CHUNK
    depth = hg_lb.shape[0]
    mod_map = (lambda b, i: (b, 0, 0)) if mod_row is None else (lambda b, i: (mod_row, 0, 0))
    lb2 = hg_lb.reshape(depth * 2, hk)

    return pl.pallas_call(
        functools.partial(_hgrn2_scan_kernel, layer, depth),
        out_shape=(jax.ShapeDtypeStruct((nb, n, hk), BF16),
                   jax.ShapeDtypeStruct((nb, n, hk), BF16),
                   jax.ShapeDtypeStruct((nb, 2, hk, dk), F32)),
        grid=(nb, nt),
        in_specs=[
            pl.BlockSpec((None, t, d), lambda b, i: (b, i, 0)),
            pl.BlockSpec((None, t, d), lambda b, i: (b, nt - 1 - i, 0)),
            pl.BlockSpec((None, 6, d), mod_map),
            _const_spec((1, d)),
            _const_spec(w_in.shape),
            _const_spec((depth * 2, hk)),
            pl.BlockSpec((None, 2, hk, dk), lambda b, i: (b, 0, 0, 0)),
        ],
        out_specs=(
            pl.BlockSpec((None, t, hk), lambda b, i: (b, i, 0)),
            pl.BlockSpec((None, t, hk), lambda b, i: (b, nt - 1 - i, 0)),
            pl.BlockSpec((None, 2, hk, dk), lambda b, i: (b, 0, 0, 0)),
        ),
        scratch_shapes=[
            pltpu.VMEM((2, hk, dk), F32),
            pltpu.VMEM((2, t, d), BF16),
            pltpu.VMEM((2, t, hk + LANES), F32),
            pltpu.VMEM((2, t, hk + LANES), F32),
            pltpu.VMEM((2, t, hk + LANES), F32),
            pltpu.VMEM((2, t, hk + LANES), BF16),
            pltpu.VMEM((2, t, 2 * hk + LANES), BF16),
            pltpu.VMEM((2, t, HG_HEADS * (dk + SCAN_CHUNK) + LANES), BF16),
            pltpu.VMEM((2, t, hk + LANES), BF16),
            pltpu.VMEM((2, t, hk + LANES), BF16),
            pltpu.VMEM((2, t, hk + LANES), BF16),
            pltpu.VMEM((2, max(nc, 8), hk), F32),
            pltpu.VMEM((2, nc, hk, dk), F32),
            pltpu.VMEM((2, nc, HG_HEADS, dk + SCAN_CHUNK, dk), BF16),
        ],
        compiler_params=pltpu.CompilerParams(
            dimension_semantics=("arbitrary", "arbitrary"),
            vmem_limit_bytes=V7X_VMEM_LIMIT_BYTES),
        name="hgrn2_scan",
    )(x, x, mod, norm_w.reshape(1, d), w_in, lb2, s0)


def _hgrn2_ffn_kernel(final_norm, g_off, x_ref, of_ref, ob_ref, mod_ref, nwm_ref, wg_ref, hnw_ref, wo_ref,
                      nwf_ref, win_ref, wout_ref, *rest):
    if final_norm:
        fnw_ref, o_ref, h_scr, og_scr, g_scr = rest
    else:
        o_ref, h_scr, og_scr, g_scr = rest
    x = x_ref[...]
    hv = of_ref.shape[-1]
    dv = hv // HG_HEADS
    d = x.shape[-1]
    h_scr[...] = _ada_norm(x, nwm_ref[...], mod_ref[0:1, :], mod_ref[1:2, :]).astype(BF16)
    for j in range(hv // COL_BLOCK):
        gate = _dot(h_scr[...], wg_ref[:, g_off + j * COL_BLOCK:g_off + (j + 1) * COL_BLOCK])
        gate = _silu(gate)
        for hh in range(j * COL_BLOCK // dv, (j + 1) * COL_BLOCK // dv):
            cs = slice(hh * dv, (hh + 1) * dv)
            oh = of_ref[:, cs].astype(F32) + ob_ref[:, cs].astype(F32)
            on = _rms_norm(oh, hnw_ref[:, cs])
            og_scr[:, cs] = (on * gate[:, hh * dv - j * COL_BLOCK:(hh + 1) * dv - j * COL_BLOCK]).astype(BF16)
    y = _dot(og_scr[...], wo_ref[:, :d])
    x = x + mod_ref[2:3, :] * y
    x = _swiglu_residual(x, mod_ref, nwf_ref, win_ref, wout_ref, h_scr, g_scr)
    if final_norm:
        x = _rms_norm(x, fnw_ref[...])
    o_ref[...] = x


def _hgrn2_ffn(x, o_f, o_b, mod, mod_row, norm_mix_w, hg_w_in, g_off, hg_norm_w, hg_w_out, norm_ffn_w,
               w_in, w_out, final_norm_w, tm):
    nb, n, d = x.shape
    hv = o_f.shape[-1]
    d_ff = w_out.shape[0]
    final_norm = final_norm_w is not None
    mod_map = (lambda b, i: (b, 0, 0)) if mod_row is None else (lambda b, i: (mod_row, 0, 0))
    in_specs = [
        pl.BlockSpec((None, tm, d), lambda b, i: (b, i, 0)),
        pl.BlockSpec((None, tm, hv), lambda b, i: (b, i, 0)),
        pl.BlockSpec((None, tm, hv), lambda b, i: (b, i, 0)),
        pl.BlockSpec((None, 6, d), mod_map),
        _const_spec((1, d)),
        _const_spec(hg_w_in.shape),
        _const_spec((1, hv)),
        _const_spec(hg_w_out.shape),
        _const_spec((1, d)),
        _const_spec(w_in.shape),
        _const_spec(w_out.shape),
    ]
    args = [x, o_f, o_b, mod, norm_mix_w.reshape(1, d), hg_w_in, hg_norm_w.reshape(1, hv), hg_w_out,
            norm_ffn_w.reshape(1, d), w_in, w_out]
    if final_norm:
        in_specs.append(_const_spec((1, d)))
        args.append(final_norm_w.reshape(1, d))
    return pl.pallas_call(
        functools.partial(_hgrn2_ffn_kernel, final_norm, g_off),
        out_shape=jax.ShapeDtypeStruct((nb, n, d), F32),
        grid=(nb, n // tm),
        in_specs=in_specs,
        out_specs=pl.BlockSpec((None, tm, d), lambda b, i: (b, i, 0)),
        scratch_shapes=[pltpu.VMEM((tm, d), BF16), pltpu.VMEM((tm, hv), BF16), pltpu.VMEM((tm, d_ff), BF16)],
        compiler_params=pltpu.CompilerParams(
            dimension_semantics=("arbitrary", "arbitrary"),
            vmem_limit_bytes=V7X_VMEM_LIMIT_BYTES),
        name="hgrn2_ffn",
    )(*args)


def _sincos(pos, dim):
    half = dim // 2
    omega = 1.0 / (POS_BASE ** (jnp.arange(half, dtype=F32) / half))
    ang = pos.astype(F32)[:, None] * omega[None, :]
    return jnp.concatenate([jnp.sin(ang), jnp.cos(ang)], axis=-1)


def _grid_pos_code(n, d):
    rows = n // GRID_W
    half = d // 2
    row_code = _sincos(jnp.arange(rows), half)
    col_code = _sincos(jnp.arange(GRID_W), half)
    code = jnp.concatenate([
        jnp.broadcast_to(row_code[:, None, :], (rows, GRID_W, half)),
        jnp.broadcast_to(col_code[None, :, :], (rows, GRID_W, half))], axis=-1)
    return code.reshape(rows * GRID_W, d)


def _mxu_weight(w):
    w = w.astype(BF16)
    if (w.shape[-1] // LANES) % STRIDED_LOAD_PERIOD == 0:
        w = jnp.pad(w, ((0, 0), (0, LANES)))
    return w


def _pick_tile(n, target):
    t = min(n, target)
    while n % t:
        t //= 2
    return t


def kernel(x, c, ctx, c_ctx, ada_w, ada_b, norm_mix_w, norm_ffn_w, gm_w_in, gm_b_in, gm_ln_g, gm_ln_b,
           gm_w_s, gm_b_s, gm_w_out, hg_w_in, hg_lb, hg_norm_w, hg_w_out, ffn_w_in, ffn_w_out, final_norm_w):
    bsz, n, d = x.shape
    n_ctx = ctx.shape[1]
    depth = ada_w.shape[0]
    assert bsz + 1 <= MOD_ROWS
    hk = hg_lb.shape[-1]
    dk = hk // HG_HEADS

    cvec = jnp.zeros((MOD_ROWS, d), F32).at[:bsz].set(c).at[bsz].set(c_ctx)
    mod_all = _ada_mod(cvec, ada_w, ada_b).reshape(depth, MOD_ROWS, 6, d)
    ctx_row = bsz

    pos = _grid_pos_code(n, d)
    ctx_flat = ctx.reshape(1, bsz * n_ctx, d)

    tm_lat = _pick_tile(n, 256)
    tm_ffn = _pick_tile(n, 512)
    tm_ctx = _pick_tile(n_ctx, 256)
    t_scan = _pick_tile(n, 256)

    for i in range(depth):
        last = i == depth - 1
        use_a = i % N_MIXERS == 0
        j = i // N_MIXERS
        mod = mod_all[i]
        ffn_in = _mxu_weight(ffn_w_in[i])
        ffn_out = _mxu_weight(ffn_w_out[i])

        if use_a:
            half = gm_ln_g.shape[-1]
            gd = half // GM_GROUPS
            gm = (norm_mix_w[i], _mxu_weight(gm_w_in[j]), gm_b_in[j], gm_ln_g[j], gm_ln_b[j],
                  gm_w_s[j].astype(BF16),
                  jnp.broadcast_to(gm_b_s[j][:, :, None], (GM_GROUPS, GM_CHUNK, gd)),
                  _mxu_weight(gm_w_out[j]))
            x = _gmlp_mixer(x, pos if i == 0 else None, mod, None, *gm, tm=tm_lat)
            x = _swiglu_ffn(x, mod, None, norm_ffn_w[i], ffn_in, ffn_out,
                            final_norm_w if last else None, tm=tm_ffn)
            if not last:
                ctx_flat = _gmlp_mixer(ctx_flat, None, mod, ctx_row, *gm, tm=tm_ctx)
                ctx_flat = _swiglu_ffn(ctx_flat, mod, ctx_row, norm_ffn_w[i], ffn_in, ffn_out, None, tm=tm_ctx)
        else:
            if i == 0:
                x = x + pos
            w = _mxu_weight(hg_w_in[j])
            scan = functools.partial(_hgrn2_scan, norm_w=norm_mix_w[i], w_in=w, hg_lb=hg_lb, layer=i)
            zero = jnp.zeros((bsz, 2, hk, dk), F32)
            ctx3 = ctx_flat.reshape(bsz, n_ctx, d)
            oc_f, oc_b, s_ctx = scan(ctx3, mod, ctx_row, s0=zero, t=n_ctx)
            o_f, o_b, _ = scan(x, mod, None, s0=s_ctx, t=t_scan)
            readout = functools.partial(_hgrn2_ffn, norm_mix_w=norm_mix_w[i], hg_w_in=w, g_off=4 * hk,
                                        hg_norm_w=hg_norm_w[j], hg_w_out=_mxu_weight(hg_w_out[j]),
                                        norm_ffn_w=norm_ffn_w[i], w_in=ffn_in, w_out=ffn_out)
            x = readout(x, o_f, o_b, mod, None, final_norm_w=final_norm_w if last else None, tm=tm_ffn)
            if not last:
                ctx3 = readout(ctx3, oc_f, oc_b, mod, ctx_row, final_norm_w=None, tm=tm_ctx)
                ctx_flat = ctx3.reshape(1, bsz * n_ctx, d)
    return x
```

```python
import functools

import jax
import jax.numpy as jnp
from jax import lax
from jax.experimental import pallas as pl
from jax.experimental.pallas import tpu as pltpu

NORM_EPS = 1e-6
POS_BASE = 10000.0
GRID_W = 64
N_MIXERS = 2
GM_CHUNK = 128
GM_GROUPS = 8
GROUPS_PER_DOT = 2
HG_HEADS = 8
SCAN_CHUNK = 128
MOD_ROWS = 16

V7X_VMEM_LIMIT_BYTES = 56 * 1024 * 1024
COL_BLOCK = 256
LANES = 128
STRIDED_LOAD_PERIOD = 8

BF16 = jnp.bfloat16
F32 = jnp.float32


def _dot(a, b):
    return jnp.dot(a, b, preferred_element_type=F32)


def _dot_nt(a, b):
    return lax.dot_general(a, b, (((1,), (1,)), ((), ())), preferred_element_type=F32)


def _dot_tn(a, b):
    return lax.dot_general(a, b, (((0,), (0,)), ((), ())), preferred_element_type=F32)


def _ada_norm(x, w, shift, scale):
    ms = jnp.mean(x * x, axis=-1, keepdims=True)
    return (x * lax.rsqrt(ms + NORM_EPS) * w) * (1.0 + scale) + shift


def _silu(x):
    return x * (0.5 + 0.5 * jnp.tanh(0.5 * x))


def _gelu_exact(x):
    return 0.5 * x * (1.0 + lax.erf(x * (2.0 ** -0.5)))


def _const_spec(shape):
    return pl.BlockSpec(shape, lambda *_: (0,) * len(shape), pipeline_mode=pl.Buffered(1))


def _ada_mod_kernel(c_ref, w_ref, b_ref, o_ref):
    c = c_ref[...]
    s = c * jax.nn.sigmoid(c)
    o_ref[...] = jnp.dot(s, w_ref[...], preferred_element_type=F32,
                         precision=lax.Precision.HIGHEST) + b_ref[...]


def _ada_mod(cvec, ada_w, ada_b):
    depth, d, n6 = ada_w.shape
    tn = 1536
    return pl.pallas_call(
        _ada_mod_kernel,
        out_shape=jax.ShapeDtypeStruct((depth, MOD_ROWS, n6), F32),
        grid=(depth, n6 // tn),
        in_specs=[
            pl.BlockSpec((MOD_ROWS, d), lambda l, j: (0, 0)),
            pl.BlockSpec((None, d, tn), lambda l, j: (l, 0, j)),
            pl.BlockSpec((None, 1, tn), lambda l, j: (l, 0, j)),
        ],
        out_specs=pl.BlockSpec((None, MOD_ROWS, tn), lambda l, j: (l, 0, j)),
        compiler_params=pltpu.CompilerParams(
            dimension_semantics=("arbitrary", "arbitrary"),
            vmem_limit_bytes=V7X_VMEM_LIMIT_BYTES),
        name="ada_mod",
    )(cvec, ada_w, ada_b.reshape(depth, 1, n6))


def _gmlp_kernel(add_pos, *refs):
    if add_pos:
        x_ref, pos_ref = refs[0], refs[1]
        refs = refs[2:]
    else:
        x_ref, pos_ref = refs[0], None
        refs = refs[1:]
    (mod_ref, nw_ref, win_ref, bin_ref, lng_ref, lnb_ref, ws_ref, bs_ref, wout_ref,
     o_ref, h_scr, z_scr, vn_scr, p_scr) = refs
    tm, d = x_ref.shape
    half = lng_ref.shape[-1]
    gd = half // GM_GROUPS
    n_chunks = tm // GM_CHUNK

    x = x_ref[...]
    if add_pos:
        x = x + pos_ref[...]
    h_scr[...] = _ada_norm(x, nw_ref[...], mod_ref[0:1, :], mod_ref[1:2, :]).astype(BF16)

    def z_block(j):
        cs = slice(j * COL_BLOCK, (j + 1) * COL_BLOCK)
        z = _dot(h_scr[...], win_ref[:, cs]) + bin_ref[:, cs]
        z_scr[:, cs] = _gelu_exact(z)

    n_blk = half // COL_BLOCK
    for j in range(n_blk, 2 * n_blk):
        z_block(j)
    v = z_scr[:, half:2 * half]
    mu = jnp.mean(v, axis=-1, keepdims=True)
    vc = v - mu
    rstd = lax.rsqrt(jnp.mean(vc * vc, axis=-1, keepdims=True) + NORM_EPS)
    for j in range(n_blk):
        z_block(j)

    y = None
    for g in range(GM_GROUPS):
        ucs = slice(g * gd, (g + 1) * gd)
        vcs = slice(half + g * gd, half + (g + 1) * gd)
        vn_scr[:, ucs] = ((z_scr[:, vcs] - mu) * rstd * lng_ref[:, ucs] + lnb_ref[:, ucs]).astype(BF16)
        for c in range(n_chunks):
            rs = slice(c * GM_CHUNK, (c + 1) * GM_CHUNK)
            vs = _dot(ws_ref[g], vn_scr[rs, ucs]) + bs_ref[g]
            p_scr[rs, ucs] = (z_scr[rs, ucs] * vs).astype(BF16)
        if (g + 1) % GROUPS_PER_DOT == 0:
            ks = slice((g + 1 - GROUPS_PER_DOT) * gd, (g + 1) * gd)
            part = _dot(p_scr[:, ks], wout_ref[ks, :d])
            y = part if y is None else y + part
    o_ref[...] = x + mod_ref[2:3, :] * y


def _gmlp_mixer(x, pos, mod, mod_row, norm_w, w_in, b_in, ln_g, ln_b, w_s, b_s, w_out, tm):
    nb, n, d = x.shape
    half = ln_g.shape[-1]
    gd = half // GM_GROUPS
    add_pos = pos is not None
    mod_map = (lambda b, i: (b, 0, 0)) if mod_row is None else (lambda b, i: (mod_row, 0, 0))
    in_specs = [pl.BlockSpec((None, tm, d), lambda b, i: (b, i, 0))]
    args = [x]
    if add_pos:
        in_specs.append(pl.BlockSpec((tm, d), lambda b, i: (i, 0)))
        args.append(pos)
    in_specs += [
        pl.BlockSpec((None, 6, d), mod_map),
        _const_spec((1, d)),
        _const_spec(w_in.shape),
        _const_spec((1, 2 * half)),
        _const_spec((1, half)),
        _const_spec((1, half)),
        _const_spec((GM_GROUPS, GM_CHUNK, GM_CHUNK)),
        _const_spec((GM_GROUPS, GM_CHUNK, gd)),
        _const_spec(w_out.shape),
    ]
    args += [mod, norm_w.reshape(1, d), w_in, b_in.reshape(1, -1), ln_g.reshape(1, -1), ln_b.reshape(1, -1),
             w_s, b_s, w_out]
    return pl.pallas_call(
        functools.partial(_gmlp_kernel, add_pos),
        out_shape=jax.ShapeDtypeStruct((nb, n, d), F32),
        grid=(nb, n // tm),
        in_specs=in_specs,
        out_specs=pl.BlockSpec((None, tm, d), lambda b, i: (b, i, 0)),
        scratch_shapes=[pltpu.VMEM((tm, d), BF16), pltpu.VMEM((tm, 2 * half + LANES), F32),
                        pltpu.VMEM((tm, half + LANES), BF16), pltpu.VMEM((tm, half + LANES), BF16)],
        compiler_params=pltpu.CompilerParams(
            dimension_semantics=("arbitrary", "arbitrary"),
            vmem_limit_bytes=V7X_VMEM_LIMIT_BYTES),
        name="gmlp_mixer",
    )(*args)


def _swiglu_residual(x, mod_ref, nw_ref, win_ref, wout_ref, h_scr, g_scr):
    d = x.shape[-1]
    d_ff = g_scr.shape[-1]
    h_scr[...] = _ada_norm(x, nw_ref[...], mod_ref[3:4, :], mod_ref[4:5, :]).astype(BF16)
    for j in range(d_ff // COL_BLOCK):
        cs = slice(j * COL_BLOCK, (j + 1) * COL_BLOCK)
        a = _dot(h_scr[...], win_ref[:, cs])
        b = _dot(h_scr[...], win_ref[:, d_ff + j * COL_BLOCK:d_ff + (j + 1) * COL_BLOCK])
        g_scr[:, cs] = (_silu(a) * b).astype(BF16)
    y = _dot(g_scr[...], wout_ref[:, :d])
    return x + mod_ref[5:6, :] * y


def _rms_norm(x, w):
    ms = jnp.mean(x * x, axis=-1, keepdims=True)
    return x * lax.rsqrt(ms + NORM_EPS) * w


def _ffn_kernel(final_norm, x_ref, mod_ref, nw_ref, win_ref, wout_ref, *rest):
    if final_norm:
        fnw_ref, o_ref, h_scr, g_scr = rest
    else:
        o_ref, h_scr, g_scr = rest
    x = _swiglu_residual(x_ref[...], mod_ref, nw_ref, win_ref, wout_ref, h_scr, g_scr)
    if final_norm:
        x = _rms_norm(x, fnw_ref[...])
    o_ref[...] = x


def _swiglu_ffn(x, mod, mod_row, norm_w, w_in, w_out, final_norm_w, tm):
    nb, n, d = x.shape
    d_ff = w_out.shape[0]
    final_norm = final_norm_w is not None
    mod_map = (lambda b, i: (b, 0, 0)) if mod_row is None else (lambda b, i: (mod_row, 0, 0))
    in_specs = [
        pl.BlockSpec((None, tm, d), lambda b, i: (b, i, 0)),
        pl.BlockSpec((None, 6, d), mod_map),
        _const_spec((1, d)),
        _const_spec(w_in.shape),
        _const_spec(w_out.shape),
    ]
    args = [x, mod, norm_w.reshape(1, d), w_in, w_out]
    if final_norm:
        in_specs.append(_const_spec((1, d)))
        args.append(final_norm_w.reshape(1, d))
    return pl.pallas_call(
        functools.partial(_ffn_kernel, final_norm),
        out_shape=jax.ShapeDtypeStruct((nb, n, d), F32),
        grid=(nb, n // tm),
        in_specs=in_specs,
        out_specs=pl.BlockSpec((None, tm, d), lambda b, i: (b, i, 0)),
        scratch_shapes=[pltpu.VMEM((tm, d), BF16), pltpu.VMEM((tm, d_ff), BF16)],
        compiler_params=pltpu.CompilerParams(
            dimension_semantics=("arbitrary", "arbitrary"),
            vmem_limit_bytes=V7X_VMEM_LIMIT_BYTES),
        name="swiglu_ffn",
    )(*args)


def _lower_bounds(lb_ref, layer, depth):
    out = []
    for dirn in range(2):
        logits = [lb_ref[2 * l + dirn:2 * l + dirn + 1, :] for l in range(depth)]
        m = functools.reduce(jnp.maximum, logits)
        e = [jnp.exp(t - m) for t in logits]
        denom = functools.reduce(lambda a, b: a + b, e)
        num = jnp.zeros_like(m)
        for l in range(1, layer + 1):
            num = num + e[l]
        out.append(num / denom)
    return out


LOG2_E = 1.4426950408889634


def _forget(z, lb):
    t = jnp.exp(-jnp.abs(z))
    r = 1.0 / (1.0 + t)
    tr = t * r
    pos = z >= 0.0
    one_m = 1.0 - lb
    f = lb + one_m * jnp.where(pos, r, tr)
    key = one_m * jnp.where(pos, tr, r)
    return key, jnp.log(f) * LOG2_E


def _scan_tile(q_scr, k_scr, lf_scr, v_scr, hm_scr, lq_scr, qa_scr, ki_scr, kd_scr, dec_scr, u_scr, rv_scr,
               st_ref, o_ref, rev):
    t = q_scr.shape[0]
    hk, dk = st_ref.shape
    pitch = dk + SCAN_CHUNK
    heads_per_block = COL_BLOCK // dk
    n_chunks = t // SCAN_CHUNK
    row = lax.broadcasted_iota(jnp.int32, (SCAN_CHUNK, SCAN_CHUNK), 0)
    col = lax.broadcasted_iota(jnp.int32, (SCAN_CHUNK, SCAN_CHUNK), 1)
    tri = (row <= col) if rev else (row >= col)
    tri_bf = tri.astype(BF16)
    last = 0 if rev else SCAN_CHUNK - 1
    mid = SCAN_CHUNK // 2 if rev else SCAN_CHUNK // 2 - 1
    order = range(n_chunks - 1, -1, -1) if rev else range(n_chunks)
    rows = [slice(c * SCAN_CHUNK, (c + 1) * SCAN_CHUNK) for c in range(n_chunks)]
    cols = [slice(hh * dk, (hh + 1) * dk) for hh in range(HG_HEADS)]

    def decays():
        for c in order:
            rs = rows[c]
            for j in range(hk // COL_BLOCK):
                cb = slice(j * COL_BLOCK, (j + 1) * COL_BLOCK)
                cb2 = slice(hk + j * COL_BLOCK, hk + (j + 1) * COL_BLOCK)
                lf = lf_scr[rs, cb]
                hi = lf.astype(BF16)
                hm_scr[rs, cb] = hi
                hm_scr[rs, cb2] = (lf - hi.astype(F32)).astype(BF16)
                b = _dot(tri_bf, hm_scr[rs, cb]) + _dot(tri_bf, hm_scr[rs, cb2])
                b_last = b[last:last + 1, :]
                b_mid = b[mid:mid + 1, :]
                kf = k_scr[rs, cb]
                qf = q_scr[rs, cb]
                qa = qf * jnp.exp2(b - b_mid)
                ki = kf * jnp.exp2(b_mid - b)
                qd = (qa * jnp.exp2(b_mid)).astype(BF16)
                for k in range(heads_per_block):
                    hh = j * heads_per_block + k
                    lq_scr[rs, hh * pitch:hh * pitch + dk] = qd[:, k * dk:(k + 1) * dk]
                qa_scr[rs, cb] = qa.astype(BF16)
                ki_scr[rs, cb] = ki.astype(BF16)
                kd_scr[rs, cb] = (ki * jnp.exp2(b_last - b_mid)).astype(BF16)
                dec_scr[c:c + 1, cb] = jnp.exp2(b_last)

    def scores():
        for c in order:
            rs = rows[c]
            for hh, cs in enumerate(cols):
                att = _dot_nt(qa_scr[rs, cs], ki_scr[rs, cs])
                lq_scr[rs, hh * pitch + dk:hh * pitch + dk + SCAN_CHUNK] = jnp.where(tri, att, 0.0).astype(BF16)
                rv_scr[c, hh, dk:, :] = v_scr[rs, cs]
                u_scr[c, cs, :] = _dot_tn(v_scr[rs, cs], kd_scr[rs, cs])

    def recurrence():
        for c in order:
            for hh, cs in enumerate(cols):
                st = st_ref[cs, :]
                rv_scr[c, hh, :dk, :] = st.T.astype(BF16)
                st_ref[cs, :] = st * dec_scr[c:c + 1, cs] + u_scr[c, cs, :]

    def outputs():
        for c in order:
            rs = rows[c]
            for hh, cs in enumerate(cols):
                o = _dot(lq_scr[rs, hh * pitch:hh * pitch + dk + SCAN_CHUNK], rv_scr[c, hh])
                o_ref[rs, cs] = o.astype(o_ref.dtype)

    return decays, scores, recurrence, outputs


def _hgrn2_scan_kernel(layer, depth, xf_ref, xb_ref, mod_ref, nw_ref, w_ref,
                       lb_ref, s0_ref, of_ref, ob_ref, sfin_ref,
                       st_scr, h_scr, q_scr, k_scr, lf_scr, v_scr, hm_scr, lq_scr, qa_scr, ki_scr, kd_scr,
                       dec_scr, u_scr, rv_scr):
    i = pl.program_id(1)

    @pl.when(i == 0)
    def _():
        st_scr[...] = s0_ref[...]

    hk = lb_ref.shape[-1]
    lbs = _lower_bounds(lb_ref, layer, depth)

    def project(dirn, x_ref, j):
        h_d = h_scr.at[dirn]
        if j == 0:
            h_d[...] = _ada_norm(x_ref[...], nw_ref[...], mod_ref[0:1, :], mod_ref[1:2, :]).astype(BF16)
        cb = slice(j * COL_BLOCK, (j + 1) * COL_BLOCK)
        q_scr[dirn, :, cb] = _silu(_dot(h_d[...], w_ref[:, cb]))
        f_off = (1 + dirn) * hk + j * COL_BLOCK
        key, log2_f = _forget(_dot(h_d[...], w_ref[:, f_off:f_off + COL_BLOCK]), lbs[dirn][:, cb])
        k_scr[dirn, :, cb] = key
        lf_scr[dirn, :, cb] = log2_f
        i_off = 3 * hk + j * COL_BLOCK
        v_scr[dirn, :, cb] = _dot(h_d[...], w_ref[:, i_off:i_off + COL_BLOCK]).astype(BF16)

    for dirn, (x_ref, o_ref) in enumerate(((xf_ref, of_ref), (xb_ref, ob_ref))):
        decays, scores, recurrence, outputs = _scan_tile(
            q_scr.at[dirn], k_scr.at[dirn], lf_scr.at[dirn], v_scr.at[dirn], hm_scr.at[dirn], lq_scr.at[dirn],
            qa_scr.at[dirn], ki_scr.at[dirn], kd_scr.at[dirn], dec_scr.at[dirn], u_scr.at[dirn], rv_scr.at[dirn],
            st_scr.at[dirn], o_ref, rev=(dirn == 1))
        for j in range(hk // COL_BLOCK):
            project(dirn, x_ref, j)
        decays()
        scores()
        recurrence()
        outputs()

    @pl.when(i == pl.num_programs(1) - 1)
    def _():
        sfin_ref[...] = st_scr[...]


def _hgrn2_scan(x, mod, mod_row, norm_w, w_in, hg_lb, layer, s0, t):
    nb, n, d = x.shape
    hk = hg_lb.shape[-1]
    dk = hk // HG_HEADS
    nt = n // t
    nc = t // SCAN_CHUNK
    depth = hg_lb.shape[0]
    mod_map = (lambda b, i: (b, 0, 0)) if mod_row is None else (lambda b, i: (mod_row, 0, 0))
    lb2 = hg_lb.reshape(depth * 2, hk)

    return pl.pallas_call(
        functools.partial(_hgrn2_scan_kernel, layer, depth),
        out_shape=(jax.ShapeDtypeStruct((nb, n, hk), BF16),
                   jax.ShapeDtypeStruct((nb, n, hk), BF16),
                   jax.ShapeDtypeStruct((nb, 2, hk, dk), F32)),
        grid=(nb, nt),
        in_specs=[
            pl.BlockSpec((None, t, d), lambda b, i: (b, i, 0)),
            pl.BlockSpec((None, t, d), lambda b, i: (b, nt - 1 - i, 0)),
            pl.BlockSpec((None, 6, d), mod_map),
            _const_spec((1, d)),
            _const_spec(w_in.shape),
            _const_spec((depth * 2, hk)),
            pl.BlockSpec((None, 2, hk, dk), lambda b, i: (b, 0, 0, 0)),
        ],
        out_specs=(
            pl.BlockSpec((None, t, hk), lambda b, i: (b, i, 0)),
            pl.BlockSpec((None, t, hk), lambda b, i: (b, nt - 1 - i, 0)),
            pl.BlockSpec((None, 2, hk, dk), lambda b, i: (b, 0, 0, 0)),
        ),
        scratch_shapes=[
            pltpu.VMEM((2, hk, dk), F32),
            pltpu.VMEM((2, t, d), BF16),
            pltpu.VMEM((2, t, hk + LANES), F32),
            pltpu.VMEM((2, t, hk + LANES), F32),
            pltpu.VMEM((2, t, hk + LANES), F32),
            pltpu.VMEM((2, t, hk + LANES), BF16),
            pltpu.VMEM((2, t, 2 * hk + LANES), BF16),
            pltpu.VMEM((2, t, HG_HEADS * (dk + SCAN_CHUNK) + LANES), BF16),
            pltpu.VMEM((2, t, hk + LANES), BF16),
            pltpu.VMEM((2, t, hk + LANES), BF16),
            pltpu.VMEM((2, t, hk + LANES), BF16),
            pltpu.VMEM((2, max(nc, 8), hk), F32),
            pltpu.VMEM((2, nc, hk, dk), F32),
            pltpu.VMEM((2, nc, HG_HEADS, dk + SCAN_CHUNK, dk), BF16),
        ],
        compiler_params=pltpu.CompilerParams(
            dimension_semantics=("arbitrary", "arbitrary"),
            vmem_limit_bytes=V7X_VMEM_LIMIT_BYTES),
        name="hgrn2_scan",
    )(x, x, mod, norm_w.reshape(1, d), w_in, lb2, s0)


def _hgrn2_ffn_kernel(final_norm, g_off, x_ref, of_ref, ob_ref, mod_ref, nwm_ref, wg_ref, hnw_ref, wo_ref,
                      nwf_ref, win_ref, wout_ref, *rest):
    if final_norm:
        fnw_ref, o_ref, h_scr, og_scr, g_scr = rest
    else:
        o_ref, h_scr, og_scr, g_scr = rest
    x = x_ref[...]
    hv = of_ref.shape[-1]
    dv = hv // HG_HEADS
    d = x.shape[-1]
    h_scr[...] = _ada_norm(x, nwm_ref[...], mod_ref[0:1, :], mod_ref[1:2, :]).astype(BF16)
    for j in range(hv // COL_BLOCK):
        gate = _dot(h_scr[...], wg_ref[:, g_off + j * COL_BLOCK:g_off + (j + 1) * COL_BLOCK])
        gate = _silu(gate)
        for hh in range(j * COL_BLOCK // dv, (j + 1) * COL_BLOCK // dv):
            cs = slice(hh * dv, (hh + 1) * dv)
            oh = of_ref[:, cs].astype(F32) + ob_ref[:, cs].astype(F32)
            on = _rms_norm(oh, hnw_ref[:, cs])
            og_scr[:, cs] = (on * gate[:, hh * dv - j * COL_BLOCK:(hh + 1) * dv - j * COL_BLOCK]).astype(BF16)
    y = _dot(og_scr[...], wo_ref[:, :d])
    x = x + mod_ref[2:3, :] * y
    x = _swiglu_residual(x, mod_ref, nwf_ref, win_ref, wout_ref, h_scr, g_scr)
    if final_norm:
        x = _rms_norm(x, fnw_ref[...])
    o_ref[...] = x


def _hgrn2_ffn(x, o_f, o_b, mod, mod_row, norm_mix_w, hg_w_in, g_off, hg_norm_w, hg_w_out, norm_ffn_w,
               w_in, w_out, final_norm_w, tm):
    nb, n, d = x.shape
    hv = o_f.shape[-1]
    d_ff = w_out.shape[0]
    final_norm = final_norm_w is not None
    mod_map = (lambda b, i: (b, 0, 0)) if mod_row is None else (lambda b, i: (mod_row, 0, 0))
    in_specs = [
        pl.BlockSpec((None, tm, d), lambda b, i: (b, i, 0)),
        pl.BlockSpec((None, tm, hv), lambda b, i: (b, i, 0)),
        pl.BlockSpec((None, tm, hv), lambda b, i: (b, i, 0)),
        pl.BlockSpec((None, 6, d), mod_map),
        _const_spec((1, d)),
        _const_spec(hg_w_in.shape),
        _const_spec((1, hv)),
        _const_spec(hg_w_out.shape),
        _const_spec((1, d)),
        _const_spec(w_in.shape),
        _const_spec(w_out.shape),
    ]
    args = [x, o_f, o_b, mod, norm_mix_w.reshape(1, d), hg_w_in, hg_norm_w.reshape(1, hv), hg_w_out,
            norm_ffn_w.reshape(1, d), w_in, w_out]
    if final_norm:
        in_specs.append(_const_spec((1, d)))
        args.append(final_norm_w.reshape(1, d))
    return pl.pallas_call(
        functools.partial(_hgrn2_ffn_kernel, final_norm, g_off),
        out_shape=jax.ShapeDtypeStruct((nb, n, d), F32),
        grid=(nb, n // tm),
        in_specs=in_specs,
        out_specs=pl.BlockSpec((None, tm, d), lambda b, i: (b, i, 0)),
        scratch_shapes=[pltpu.VMEM((tm, d), BF16), pltpu.VMEM((tm, hv), BF16), pltpu.VMEM((tm, d_ff), BF16)],
        compiler_params=pltpu.CompilerParams(
            dimension_semantics=("arbitrary", "arbitrary"),
            vmem_limit_bytes=V7X_VMEM_LIMIT_BYTES),
        name="hgrn2_ffn",
    )(*args)


def _sincos(pos, dim):
    half = dim // 2
    omega = 1.0 / (POS_BASE ** (jnp.arange(half, dtype=F32) / half))
    ang = pos.astype(F32)[:, None] * omega[None, :]
    return jnp.concatenate([jnp.sin(ang), jnp.cos(ang)], axis=-1)


def _grid_pos_code(n, d):
    rows = n // GRID_W
    half = d // 2
    row_code = _sincos(jnp.arange(rows), half)
    col_code = _sincos(jnp.arange(GRID_W), half)
    code = jnp.concatenate([
        jnp.broadcast_to(row_code[:, None, :], (rows, GRID_W, half)),
        jnp.broadcast_to(col_code[None, :, :], (rows, GRID_W, half))], axis=-1)
    return code.reshape(rows * GRID_W, d)


def _mxu_weight(w):
    w = w.astype(BF16)
    if (w.shape[-1] // LANES) % STRIDED_LOAD_PERIOD == 0:
        w = jnp.pad(w, ((0, 0), (0, LANES)))
    return w


def _pick_tile(n, target):
    t = min(n, target)
    while n % t:
        t //= 2
    return t


def kernel(x, c, ctx, c_ctx, ada_w, ada_b, norm_mix_w, norm_ffn_w, gm_w_in, gm_b_in, gm_ln_g, gm_ln_b,
           gm_w_s, gm_b_s, gm_w_out, hg_w_in, hg_lb, hg_norm_w, hg_w_out, ffn_w_in, ffn_w_out, final_norm_w):
    bsz, n, d = x.shape
    n_ctx = ctx.shape[1]
    depth = ada_w.shape[0]
    assert bsz + 1 <= MOD_ROWS
    hk = hg_lb.shape[-1]
    dk = hk // HG_HEADS

    cvec = jnp.zeros((MOD_ROWS, d), F32).at[:bsz].set(c).at[bsz].set(c_ctx)
    mod_all = _ada_mod(cvec, ada_w, ada_b).reshape(depth, MOD_ROWS, 6, d)
    ctx_row = bsz

    pos = _grid_pos_code(n, d)
    ctx_flat = ctx.reshape(1, bsz * n_ctx, d)

    tm_lat = _pick_tile(n, 256)
    tm_ffn = _pick_tile(n, 512)
    tm_ctx = _pick_tile(n_ctx, 256)
    t_scan = _pick_tile(n, 256)

    for i in range(depth):
        last = i == depth - 1
        use_a = i % N_MIXERS == 0
        j = i // N_MIXERS
        mod = mod_all[i]
        ffn_in = _mxu_weight(ffn_w_in[i])
        ffn_out = _mxu_weight(ffn_w_out[i])

        if use_a:
            half = gm_ln_g.shape[-1]
            gd = half // GM_GROUPS
            gm = (norm_mix_w[i], _mxu_weight(gm_w_in[j]), gm_b_in[j], gm_ln_g[j], gm_ln_b[j],
                  gm_w_s[j].astype(BF16),
                  jnp.broadcast_to(gm_b_s[j][:, :, None], (GM_GROUPS, GM_CHUNK, gd)),
                  _mxu_weight(gm_w_out[j]))
            x = _gmlp_mixer(x, pos if i == 0 else None, mod, None, *gm, tm=tm_lat)
            x = _swiglu_ffn(x, mod, None, norm_ffn_w[i], ffn_in, ffn_out,
                            final_norm_w if last else None, tm=_pick_tile(n, 1024))
            if not last:
                ctx_flat = _gmlp_mixer(ctx_flat, None, mod, ctx_row, *gm, tm=tm_ctx)
                ctx_flat = _swiglu_ffn(ctx_flat, mod, ctx_row, norm_ffn_w[i], ffn_in, ffn_out, None, tm=tm_ctx)
        else:
            if i == 0:
                x = x + pos
            w = _mxu_weight(hg_w_in[j])
            scan = functools.partial(_hgrn2_scan, norm_w=norm_mix_w[i], w_in=w, hg_lb=hg_lb, layer=i)
            zero = jnp.zeros((bsz, 2, hk, dk), F32)
            ctx3 = ctx_flat.reshape(bsz, n_ctx, d)
            oc_f, oc_b, s_ctx = scan(ctx3, mod, ctx_row, s0=zero, t=n_ctx)
            o_f, o_b, _ = scan(x, mod, None, s0=s_ctx, t=t_scan)
            readout = functools.partial(_hgrn2_ffn, norm_mix_w=norm_mix_w[i], hg_w_in=w, g_off=4 * hk,
                                        hg_norm_w=hg_norm_w[j], hg_w_out=_mxu_weight(hg_w_out[j]),
                                        norm_ffn_w=norm_ffn_w[i], w_in=ffn_in, w_out=ffn_out)
            x = readout(x, o_f, o_b, mod, None, final_norm_w=final_norm_w if last else None, tm=tm_ffn)
            if not last:
                ctx3 = readout(ctx3, oc_f, oc_b, mod, ctx_row, final_norm_w=None, tm=tm_ctx)
                ctx_flat = ctx3.reshape(1, bsz * n_ctx, d)
    return x
```

```python
import functools

import jax
import jax.numpy as jnp
from jax import lax
from jax.experimental import pallas as pl
from jax.experimental.pallas import tpu as pltpu

NORM_EPS = 1e-6
POS_BASE = 10000.0
GRID_W = 64
N_MIXERS = 2
GM_CHUNK = 128
GM_GROUPS = 8
GROUPS_PER_DOT = 4
HG_HEADS = 8
SCAN_CHUNK = 128
MOD_ROWS = 16

V7X_VMEM_LIMIT_BYTES = 56 * 1024 * 1024
COL_BLOCK = 256
LANES = 128
STRIDED_LOAD_PERIOD = 8

BF16 = jnp.bfloat16
F32 = jnp.float32


def _dot(a, b):
    return jnp.dot(a, b, preferred_element_type=F32)


def _dot_nt(a, b):
    return lax.dot_general(a, b, (((1,), (1,)), ((), ())), preferred_element_type=F32)


def _dot_tn(a, b):
    return lax.dot_general(a, b, (((0,), (0,)), ((), ())), preferred_element_type=F32)


def _ada_norm(x, w, shift, scale):
    ms = jnp.mean(x * x, axis=-1, keepdims=True)
    return (x * lax.rsqrt(ms + NORM_EPS) * w) * (1.0 + scale) + shift


def _silu(x):
    return x * (0.5 + 0.5 * jnp.tanh(0.5 * x))


def _gelu_exact(x):
    return 0.5 * x * (1.0 + lax.erf(x * (2.0 ** -0.5)))


def _const_spec(shape):
    return pl.BlockSpec(shape, lambda *_: (0,) * len(shape), pipeline_mode=pl.Buffered(1))


def _ada_mod_kernel(c_ref, w_ref, b_ref, o_ref):
    c = c_ref[...]
    s = c * jax.nn.sigmoid(c)
    o_ref[...] = jnp.dot(s, w_ref[...], preferred_element_type=F32,
                         precision=lax.Precision.HIGHEST) + b_ref[...]


def _ada_mod(cvec, ada_w, ada_b):
    depth, d, n6 = ada_w.shape
    tn = 1536
    return pl.pallas_call(
        _ada_mod_kernel,
        out_shape=jax.ShapeDtypeStruct((depth, MOD_ROWS, n6), F32),
        grid=(depth, n6 // tn),
        in_specs=[
            pl.BlockSpec((MOD_ROWS, d), lambda l, j: (0, 0)),
            pl.BlockSpec((None, d, tn), lambda l, j: (l, 0, j)),
            pl.BlockSpec((None, 1, tn), lambda l, j: (l, 0, j)),
        ],
        out_specs=pl.BlockSpec((None, MOD_ROWS, tn), lambda l, j: (l, 0, j)),
        compiler_params=pltpu.CompilerParams(
            dimension_semantics=("arbitrary", "arbitrary"),
            vmem_limit_bytes=V7X_VMEM_LIMIT_BYTES),
        name="ada_mod",
    )(cvec, ada_w, ada_b.reshape(depth, 1, n6))


def _gmlp_kernel(add_pos, *refs):
    if add_pos:
        x_ref, pos_ref = refs[0], refs[1]
        refs = refs[2:]
    else:
        x_ref, pos_ref = refs[0], None
        refs = refs[1:]
    (mod_ref, nw_ref, win_ref, bin_ref, lng_ref, lnb_ref, ws_ref, bs_ref, wout_ref,
     o_ref, h_scr, z_scr, vn_scr, p_scr) = refs
    tm, d = x_ref.shape
    half = lng_ref.shape[-1]
    gd = half // GM_GROUPS
    n_chunks = tm // GM_CHUNK

    x = x_ref[...]
    if add_pos:
        x = x + pos_ref[...]
    h_scr[...] = _ada_norm(x, nw_ref[...], mod_ref[0:1, :], mod_ref[1:2, :]).astype(BF16)

    def z_block(j):
        cs = slice(j * COL_BLOCK, (j + 1) * COL_BLOCK)
        z = _dot(h_scr[...], win_ref[:, cs]) + bin_ref[:, cs]
        z_scr[:, cs] = _gelu_exact(z)

    n_blk = half // COL_BLOCK
    for j in range(n_blk, 2 * n_blk):
        z_block(j)
    v = z_scr[:, half:2 * half]
    mu = jnp.mean(v, axis=-1, keepdims=True)
    vc = v - mu
    rstd = lax.rsqrt(jnp.mean(vc * vc, axis=-1, keepdims=True) + NORM_EPS)
    for j in range(n_blk):
        z_block(j)

    y = None
    for g in range(GM_GROUPS):
        ucs = slice(g * gd, (g + 1) * gd)
        vcs = slice(half + g * gd, half + (g + 1) * gd)
        vn_scr[:, ucs] = ((z_scr[:, vcs] - mu) * rstd * lng_ref[:, ucs] + lnb_ref[:, ucs]).astype(BF16)
        for c in range(n_chunks):
            rs = slice(c * GM_CHUNK, (c + 1) * GM_CHUNK)
            vs = _dot(ws_ref[g], vn_scr[rs, ucs]) + bs_ref[g]
            p_scr[rs, ucs] = (z_scr[rs, ucs] * vs).astype(BF16)
        if (g + 1) % GROUPS_PER_DOT == 0:
            ks = slice((g + 1 - GROUPS_PER_DOT) * gd, (g + 1) * gd)
            part = _dot(p_scr[:, ks], wout_ref[ks, :d])
            y = part if y is None else y + part
    o_ref[...] = x + mod_ref[2:3, :] * y


def _gmlp_mixer(x, pos, mod, mod_row, norm_w, w_in, b_in, ln_g, ln_b, w_s, b_s, w_out, tm):
    nb, n, d = x.shape
    half = ln_g.shape[-1]
    gd = half // GM_GROUPS
    add_pos = pos is not None
    mod_map = (lambda b, i: (b, 0, 0)) if mod_row is None else (lambda b, i: (mod_row, 0, 0))
    in_specs = [pl.BlockSpec((None, tm, d), lambda b, i: (b, i, 0))]
    args = [x]
    if add_pos:
        in_specs.append(pl.BlockSpec((tm, d), lambda b, i: (i, 0)))
        args.append(pos)
    in_specs += [
        pl.BlockSpec((None, 6, d), mod_map),
        _const_spec((1, d)),
        _const_spec(w_in.shape),
        _const_spec((1, 2 * half)),
        _const_spec((1, half)),
        _const_spec((1, half)),
        _const_spec((GM_GROUPS, GM_CHUNK, GM_CHUNK)),
        _const_spec((GM_GROUPS, GM_CHUNK, gd)),
        _const_spec(w_out.shape),
    ]
    args += [mod, norm_w.reshape(1, d), w_in, b_in.reshape(1, -1), ln_g.reshape(1, -1), ln_b.reshape(1, -1),
             w_s, b_s, w_out]
    return pl.pallas_call(
        functools.partial(_gmlp_kernel, add_pos),
        out_shape=jax.ShapeDtypeStruct((nb, n, d), F32),
        grid=(nb, n // tm),
        in_specs=in_specs,
        out_specs=pl.BlockSpec((None, tm, d), lambda b, i: (b, i, 0)),
        scratch_shapes=[pltpu.VMEM((tm, d), BF16), pltpu.VMEM((tm, 2 * half + LANES), F32),
                        pltpu.VMEM((tm, half + LANES), BF16), pltpu.VMEM((tm, half + LANES), BF16)],
        compiler_params=pltpu.CompilerParams(
            dimension_semantics=("arbitrary", "arbitrary"),
            vmem_limit_bytes=V7X_VMEM_LIMIT_BYTES),
        name="gmlp_mixer",
    )(*args)


def _swiglu_residual(x, mod_ref, nw_ref, win_ref, wout_ref, h_scr, g_scr):
    d = x.shape[-1]
    d_ff = g_scr.shape[-1]
    h_scr[...] = _ada_norm(x, nw_ref[...], mod_ref[3:4, :], mod_ref[4:5, :]).astype(BF16)
    for j in range(d_ff // COL_BLOCK):
        cs = slice(j * COL_BLOCK, (j + 1) * COL_BLOCK)
        a = _dot(h_scr[...], win_ref[:, cs])
        b = _dot(h_scr[...], win_ref[:, d_ff + j * COL_BLOCK:d_ff + (j + 1) * COL_BLOCK])
        g_scr[:, cs] = (_silu(a) * b).astype(BF16)
    y = _dot(g_scr[...], wout_ref[:, :d])
    return x + mod_ref[5:6, :] * y


def _rms_norm(x, w):
    ms = jnp.mean(x * x, axis=-1, keepdims=True)
    return x * lax.rsqrt(ms + NORM_EPS) * w


def _ffn_kernel(final_norm, x_ref, mod_ref, nw_ref, win_ref, wout_ref, *rest):
    if final_norm:
        fnw_ref, o_ref, h_scr, g_scr = rest
    else:
        o_ref, h_scr, g_scr = rest
    x = _swiglu_residual(x_ref[...], mod_ref, nw_ref, win_ref, wout_ref, h_scr, g_scr)
    if final_norm:
        x = _rms_norm(x, fnw_ref[...])
    o_ref[...] = x


def _swiglu_ffn(x, mod, mod_row, norm_w, w_in, w_out, final_norm_w, tm):
    nb, n, d = x.shape
    d_ff = w_out.shape[0]
    final_norm = final_norm_w is not None
    mod_map = (lambda b, i: (b, 0, 0)) if mod_row is None else (lambda b, i: (mod_row, 0, 0))
    in_specs = [
        pl.BlockSpec((None, tm, d), lambda b, i: (b, i, 0)),
        pl.BlockSpec((None, 6, d), mod_map),
        _const_spec((1, d)),
        _const_spec(w_in.shape),
        _const_spec(w_out.shape),
    ]
    args = [x, mod, norm_w.reshape(1, d), w_in, w_out]
    if final_norm:
        in_specs.append(_const_spec((1, d)))
        args.append(final_norm_w.reshape(1, d))
    return pl.pallas_call(
        functools.partial(_ffn_kernel, final_norm),
        out_shape=jax.ShapeDtypeStruct((nb, n, d), F32),
        grid=(nb, n // tm),
        in_specs=in_specs,
        out_specs=pl.BlockSpec((None, tm, d), lambda b, i: (b, i, 0)),
        scratch_shapes=[pltpu.VMEM((tm, d), BF16), pltpu.VMEM((tm, d_ff), BF16)],
        compiler_params=pltpu.CompilerParams(
            dimension_semantics=("arbitrary", "arbitrary"),
            vmem_limit_bytes=V7X_VMEM_LIMIT_BYTES),
        name="swiglu_ffn",
    )(*args)


def _lower_bounds(lb_ref, layer, depth):
    out = []
    for dirn in range(2):
        logits = [lb_ref[2 * l + dirn:2 * l + dirn + 1, :] for l in range(depth)]
        m = functools.reduce(jnp.maximum, logits)
        e = [jnp.exp(t - m) for t in logits]
        denom = functools.reduce(lambda a, b: a + b, e)
        num = jnp.zeros_like(m)
        for l in range(1, layer + 1):
            num = num + e[l]
        out.append(num / denom)
    return out


LOG2_E = 1.4426950408889634


def _forget(z, lb):
    t = jnp.exp(-jnp.abs(z))
    r = 1.0 / (1.0 + t)
    tr = t * r
    pos = z >= 0.0
    one_m = 1.0 - lb
    f = lb + one_m * jnp.where(pos, r, tr)
    key = one_m * jnp.where(pos, tr, r)
    return key, jnp.log(f) * LOG2_E


def _scan_tile(q_scr, k_scr, lf_scr, v_scr, hm_scr, lq_scr, qa_scr, ki_scr, kd_scr, dec_scr, u_scr, rv_scr,
               st_ref, o_ref, rev):
    t = q_scr.shape[0]
    hk, dk = st_ref.shape
    pitch = dk + SCAN_CHUNK
    heads_per_block = COL_BLOCK // dk
    n_chunks = t // SCAN_CHUNK
    row = lax.broadcasted_iota(jnp.int32, (SCAN_CHUNK, SCAN_CHUNK), 0)
    col = lax.broadcasted_iota(jnp.int32, (SCAN_CHUNK, SCAN_CHUNK), 1)
    tri = (row <= col) if rev else (row >= col)
    tri_bf = tri.astype(BF16)
    last = 0 if rev else SCAN_CHUNK - 1
    mid = SCAN_CHUNK // 2 if rev else SCAN_CHUNK // 2 - 1
    order = range(n_chunks - 1, -1, -1) if rev else range(n_chunks)
    rows = [slice(c * SCAN_CHUNK, (c + 1) * SCAN_CHUNK) for c in range(n_chunks)]
    cols = [slice(hh * dk, (hh + 1) * dk) for hh in range(HG_HEADS)]

    def decays():
        for c in order:
            rs = rows[c]
            for j in range(hk // COL_BLOCK):
                cb = slice(j * COL_BLOCK, (j + 1) * COL_BLOCK)
                cb2 = slice(hk + j * COL_BLOCK, hk + (j + 1) * COL_BLOCK)
                lf = lf_scr[rs, cb]
                hi = lf.astype(BF16)
                hm_scr[rs, cb] = hi
                hm_scr[rs, cb2] = (lf - hi.astype(F32)).astype(BF16)
                b = _dot(tri_bf, hm_scr[rs, cb]) + _dot(tri_bf, hm_scr[rs, cb2])
                b_last = b[last:last + 1, :]
                b_mid = b[mid:mid + 1, :]
                kf = k_scr[rs, cb]
                qf = q_scr[rs, cb]
                qa = qf * jnp.exp2(b - b_mid)
                ki = kf * jnp.exp2(b_mid - b)
                qd = (qa * jnp.exp2(b_mid)).astype(BF16)
                for k in range(heads_per_block):
                    hh = j * heads_per_block + k
                    lq_scr[rs, hh * pitch:hh * pitch + dk] = qd[:, k * dk:(k + 1) * dk]
                qa_scr[rs, cb] = qa.astype(BF16)
                ki_scr[rs, cb] = ki.astype(BF16)
                kd_scr[rs, cb] = (ki * jnp.exp2(b_last - b_mid)).astype(BF16)
                dec_scr[c:c + 1, cb] = jnp.exp2(b_last)

    def scores():
        for c in order:
            rs = rows[c]
            for hh, cs in enumerate(cols):
                att = _dot_nt(qa_scr[rs, cs], ki_scr[rs, cs])
                lq_scr[rs, hh * pitch + dk:hh * pitch + dk + SCAN_CHUNK] = jnp.where(tri, att, 0.0).astype(BF16)
                rv_scr[c, hh, dk:, :] = v_scr[rs, cs]
                u_scr[c, cs, :] = _dot_tn(v_scr[rs, cs], kd_scr[rs, cs])

    def recurrence():
        for c in order:
            for hh, cs in enumerate(cols):
                st = st_ref[cs, :]
                rv_scr[c, hh, :dk, :] = st.T.astype(BF16)
                st_ref[cs, :] = st * dec_scr[c:c + 1, cs] + u_scr[c, cs, :]

    def outputs():
        for c in order:
            rs = rows[c]
            for hh, cs in enumerate(cols):
                o = _dot(lq_scr[rs, hh * pitch:hh * pitch + dk + SCAN_CHUNK], rv_scr[c, hh])
                o_ref[rs, cs] = o.astype(o_ref.dtype)

    return decays, scores, recurrence, outputs


def _hgrn2_scan_kernel(layer, depth, xf_ref, xb_ref, mod_ref, nw_ref, w_ref,
                       lb_ref, s0_ref, of_ref, ob_ref, sfin_ref,
                       st_scr, h_scr, q_scr, k_scr, lf_scr, v_scr, hm_scr, lq_scr, qa_scr, ki_scr, kd_scr,
                       dec_scr, u_scr, rv_scr):
    i = pl.program_id(1)

    @pl.when(i == 0)
    def _():
        st_scr[...] = s0_ref[...]

    hk = lb_ref.shape[-1]
    lbs = _lower_bounds(lb_ref, layer, depth)

    def project(dirn, x_ref, j):
        h_d = h_scr.at[dirn]
        if j == 0:
            h_d[...] = _ada_norm(x_ref[...], nw_ref[...], mod_ref[0:1, :], mod_ref[1:2, :]).astype(BF16)
        cb = slice(j * COL_BLOCK, (j + 1) * COL_BLOCK)
        q_scr[dirn, :, cb] = _silu(_dot(h_d[...], w_ref[:, cb]))
        f_off = (1 + dirn) * hk + j * COL_BLOCK
        key, log2_f = _forget(_dot(h_d[...], w_ref[:, f_off:f_off + COL_BLOCK]), lbs[dirn][:, cb])
        k_scr[dirn, :, cb] = key
        lf_scr[dirn, :, cb] = log2_f
        i_off = 3 * hk + j * COL_BLOCK
        v_scr[dirn, :, cb] = _dot(h_d[...], w_ref[:, i_off:i_off + COL_BLOCK]).astype(BF16)

    for dirn, (x_ref, o_ref) in enumerate(((xf_ref, of_ref), (xb_ref, ob_ref))):
        decays, scores, recurrence, outputs = _scan_tile(
            q_scr.at[dirn], k_scr.at[dirn], lf_scr.at[dirn], v_scr.at[dirn], hm_scr.at[dirn], lq_scr.at[dirn],
            qa_scr.at[dirn], ki_scr.at[dirn], kd_scr.at[dirn], dec_scr.at[dirn], u_scr.at[dirn], rv_scr.at[dirn],
            st_scr.at[dirn], o_ref, rev=(dirn == 1))
        for j in range(hk // COL_BLOCK):
            project(dirn, x_ref, j)
        decays()
        scores()
        recurrence()
        outputs()

    @pl.when(i == pl.num_programs(1) - 1)
    def _():
        sfin_ref[...] = st_scr[...]


def _hgrn2_scan(x, mod, mod_row, norm_w, w_in, hg_lb, layer, s0, t):
    nb, n, d = x.shape
    hk = hg_lb.shape[-1]
    dk = hk // HG_HEADS
    nt = n // t
    nc = t // SCAN_CHUNK
    depth = hg_lb.shape[0]
    mod_map = (lambda b, i: (b, 0, 0)) if mod_row is None else (lambda b, i: (mod_row, 0, 0))
    lb2 = hg_lb.reshape(depth * 2, hk)

    return pl.pallas_call(
        functools.partial(_hgrn2_scan_kernel, layer, depth),
        out_shape=(jax.ShapeDtypeStruct((nb, n, hk), BF16),
                   jax.ShapeDtypeStruct((nb, n, hk), BF16),
                   jax.ShapeDtypeStruct((nb, 2, hk, dk), F32)),
        grid=(nb, nt),
        in_specs=[
            pl.BlockSpec((None, t, d), lambda b, i: (b, i, 0)),
            pl.BlockSpec((None, t, d), lambda b, i: (b, nt - 1 - i, 0)),
            pl.BlockSpec((None, 6, d), mod_map),
            _const_spec((1, d)),
            _const_spec(w_in.shape),
            _const_spec((depth * 2, hk)),
            pl.BlockSpec((None, 2, hk, dk), lambda b, i: (b, 0, 0, 0)),
        ],
        out_specs=(
            pl.BlockSpec((None, t, hk), lambda b, i: (b, i, 0)),
            pl.BlockSpec((None, t, hk), lambda b, i: (b, nt - 1 - i, 0)),
            pl.BlockSpec((None, 2, hk, dk), lambda b, i: (b, 0, 0, 0)),
        ),
        scratch_shapes=[
            pltpu.VMEM((2, hk, dk), F32),
            pltpu.VMEM((2, t, d), BF16),
            pltpu.VMEM((2, t, hk + LANES), F32),
            pltpu.VMEM((2, t, hk + LANES), F32),
            pltpu.VMEM((2, t, hk + LANES), F32),
            pltpu.VMEM((2, t, hk + LANES), BF16),
            pltpu.VMEM((2, t, 2 * hk + LANES), BF16),
            pltpu.VMEM((2, t, HG_HEADS * (dk + SCAN_CHUNK) + LANES), BF16),
            pltpu.VMEM((2, t, hk + LANES), BF16),
            pltpu.VMEM((2, t, hk + LANES), BF16),
            pltpu.VMEM((2, t, hk + LANES), BF16),
            pltpu.VMEM((2, max(nc, 8), hk), F32),
            pltpu.VMEM((2, nc, hk, dk), F32),
            pltpu.VMEM((2, nc, HG_HEADS, dk + SCAN_CHUNK, dk), BF16),
        ],
        compiler_params=pltpu.CompilerParams(
            dimension_semantics=("arbitrary", "arbitrary"),
            vmem_limit_bytes=V7X_VMEM_LIMIT_BYTES),
        name="hgrn2_scan",
    )(x, x, mod, norm_w.reshape(1, d), w_in, lb2, s0)


def _hgrn2_ffn_kernel(final_norm, g_off, x_ref, of_ref, ob_ref, mod_ref, nwm_ref, wg_ref, hnw_ref, wo_ref,
                      nwf_ref, win_ref, wout_ref, *rest):
    if final_norm:
        fnw_ref, o_ref, h_scr, og_scr, g_scr = rest
    else:
        o_ref, h_scr, og_scr, g_scr = rest
    x = x_ref[...]
    hv = of_ref.shape[-1]
    dv = hv // HG_HEADS
    d = x.shape[-1]
    h_scr[...] = _ada_norm(x, nwm_ref[...], mod_ref[0:1, :], mod_ref[1:2, :]).astype(BF16)
    for j in range(hv // COL_BLOCK):
        gate = _dot(h_scr[...], wg_ref[:, g_off + j * COL_BLOCK:g_off + (j + 1) * COL_BLOCK])
        gate = _silu(gate)
        for hh in range(j * COL_BLOCK // dv, (j + 1) * COL_BLOCK // dv):
            cs = slice(hh * dv, (hh + 1) * dv)
            oh = of_ref[:, cs].astype(F32) + ob_ref[:, cs].astype(F32)
            on = _rms_norm(oh, hnw_ref[:, cs])
            og_scr[:, cs] = (on * gate[:, hh * dv - j * COL_BLOCK:(hh + 1) * dv - j * COL_BLOCK]).astype(BF16)
    y = _dot(og_scr[...], wo_ref[:, :d])
    x = x + mod_ref[2:3, :] * y
    x = _swiglu_residual(x, mod_ref, nwf_ref, win_ref, wout_ref, h_scr, g_scr)
    if final_norm:
        x = _rms_norm(x, fnw_ref[...])
    o_ref[...] = x


def _hgrn2_ffn(x, o_f, o_b, mod, mod_row, norm_mix_w, hg_w_in, g_off, hg_norm_w, hg_w_out, norm_ffn_w,
               w_in, w_out, final_norm_w, tm):
    nb, n, d = x.shape
    hv = o_f.shape[-1]
    d_ff = w_out.shape[0]
    final_norm = final_norm_w is not None
    mod_map = (lambda b, i: (b, 0, 0)) if mod_row is None else (lambda b, i: (mod_row, 0, 0))
    in_specs = [
        pl.BlockSpec((None, tm, d), lambda b, i: (b, i, 0)),
        pl.BlockSpec((None, tm, hv), lambda b, i: (b, i, 0)),
        pl.BlockSpec((None, tm, hv), lambda b, i: (b, i, 0)),
        pl.BlockSpec((None, 6, d), mod_map),
        _const_spec((1, d)),
        _const_spec(hg_w_in.shape),
        _const_spec((1, hv)),
        _const_spec(hg_w_out.shape),
        _const_spec((1, d)),
        _const_spec(w_in.shape),
        _const_spec(w_out.shape),
    ]
    args = [x, o_f, o_b, mod, norm_mix_w.reshape(1, d), hg_w_in, hg_norm_w.reshape(1, hv), hg_w_out,
            norm_ffn_w.reshape(1, d), w_in, w_out]
    if final_norm:
        in_specs.append(_const_spec((1, d)))
        args.append(final_norm_w.reshape(1, d))
    return pl.pallas_call(
        functools.partial(_hgrn2_ffn_kernel, final_norm, g_off),
        out_shape=jax.ShapeDtypeStruct((nb, n, d), F32),
        grid=(nb, n // tm),
        in_specs=in_specs,
        out_specs=pl.BlockSpec((None, tm, d), lambda b, i: (b, i, 0)),
        scratch_shapes=[pltpu.VMEM((tm, d), BF16), pltpu.VMEM((tm, hv), BF16), pltpu.VMEM((tm, d_ff), BF16)],
        compiler_params=pltpu.CompilerParams(
            dimension_semantics=("arbitrary", "arbitrary"),
            vmem_limit_bytes=V7X_VMEM_LIMIT_BYTES),
        name="hgrn2_ffn",
    )(*args)


def _sincos(pos, dim):
    half = dim // 2
    omega = 1.0 / (POS_BASE ** (jnp.arange(half, dtype=F32) / half))
    ang = pos.astype(F32)[:, None] * omega[None, :]
    return jnp.concatenate([jnp.sin(ang), jnp.cos(ang)], axis=-1)


def _grid_pos_code(n, d):
    rows = n // GRID_W
    half = d // 2
    row_code = _sincos(jnp.arange(rows), half)
    col_code = _sincos(jnp.arange(GRID_W), half)
    code = jnp.concatenate([
        jnp.broadcast_to(row_code[:, None, :], (rows, GRID_W, half)),
        jnp.broadcast_to(col_code[None, :, :], (rows, GRID_W, half))], axis=-1)
    return code.reshape(rows * GRID_W, d)


def _mxu_weight(w):
    w = w.astype(BF16)
    if (w.shape[-1] // LANES) % STRIDED_LOAD_PERIOD == 0:
        w = jnp.pad(w, ((0, 0), (0, LANES)))
    return w


def _pick_tile(n, target):
    t = min(n, target)
    while n % t:
        t //= 2
    return t


def kernel(x, c, ctx, c_ctx, ada_w, ada_b, norm_mix_w, norm_ffn_w, gm_w_in, gm_b_in, gm_ln_g, gm_ln_b,
           gm_w_s, gm_b_s, gm_w_out, hg_w_in, hg_lb, hg_norm_w, hg_w_out, ffn_w_in, ffn_w_out, final_norm_w):
    bsz, n, d = x.shape
    n_ctx = ctx.shape[1]
    depth = ada_w.shape[0]
    assert bsz + 1 <= MOD_ROWS
    hk = hg_lb.shape[-1]
    dk = hk // HG_HEADS

    cvec = jnp.zeros((MOD_ROWS, d), F32).at[:bsz].set(c).at[bsz].set(c_ctx)
    mod_all = _ada_mod(cvec, ada_w, ada_b).reshape(depth, MOD_ROWS, 6, d)
    ctx_row = bsz

    pos = _grid_pos_code(n, d)
    ctx_flat = ctx.reshape(1, bsz * n_ctx, d)

    tm_lat = _pick_tile(n, 256)
    tm_ffn = _pick_tile(n, 512)
    tm_ctx = _pick_tile(n_ctx, 256)
    t_scan = _pick_tile(n, 256)

    for i in range(depth):
        last = i == depth - 1
        use_a = i % N_MIXERS == 0
        j = i // N_MIXERS
        mod = mod_all[i]
        ffn_in = _mxu_weight(ffn_w_in[i])
        ffn_out = _mxu_weight(ffn_w_out[i])

        if use_a:
            half = gm_ln_g.shape[-1]
            gd = half // GM_GROUPS
            gm = (norm_mix_w[i], _mxu_weight(gm_w_in[j]), gm_b_in[j], gm_ln_g[j], gm_ln_b[j],
                  gm_w_s[j].astype(BF16),
                  jnp.broadcast_to(gm_b_s[j][:, :, None], (GM_GROUPS, GM_CHUNK, gd)),
                  _mxu_weight(gm_w_out[j]))
            x = _gmlp_mixer(x, pos if i == 0 else None, mod, None, *gm, tm=tm_lat)
            x = _swiglu_ffn(x, mod, None, norm_ffn_w[i], ffn_in, ffn_out,
                            final_norm_w if last else None, tm=_pick_tile(n, 1024))
            if not last:
                ctx_flat = _gmlp_mixer(ctx_flat, None, mod, ctx_row, *gm, tm=tm_ctx)
                ctx_flat = _swiglu_ffn(ctx_flat, mod, ctx_row, norm_ffn_w[i], ffn_in, ffn_out, None, tm=tm_ctx)
        else:
            if i == 0:
                x = x + pos
            w = _mxu_weight(hg_w_in[j])
            scan = functools.partial(_hgrn2_scan, norm_w=norm_mix_w[i], w_in=w, hg_lb=hg_lb, layer=i)
            zero = jnp.zeros((bsz, 2, hk, dk), F32)
            ctx3 = ctx_flat.reshape(bsz, n_ctx, d)
            oc_f, oc_b, s_ctx = scan(ctx3, mod, ctx_row, s0=zero, t=n_ctx)
            o_f, o_b, _ = scan(x, mod, None, s0=s_ctx, t=t_scan)
            readout = functools.partial(_hgrn2_ffn, norm_mix_w=norm_mix_w[i], hg_w_in=w, g_off=4 * hk,
                                        hg_norm_w=hg_norm_w[j], hg_w_out=_mxu_weight(hg_w_out[j]),
                                        norm_ffn_w=norm_ffn_w[i], w_in=ffn_in, w_out=ffn_out)
            x = readout(x, o_f, o_b, mod, None, final_norm_w=final_norm_w if last else None, tm=tm_ffn)
            if not last:
                ctx3 = readout(ctx3, oc_f, oc_b, mod, ctx_row, final_norm_w=None, tm=tm_ctx)
                ctx_flat = ctx3.reshape(1, bsz * n_ctx, d)
    return x
```
